```python
import math
import jax, jax.numpy as jnp
from jax import lax
import numpy as np

D_MODEL = 1024
BATCH = 4
SEQ = 4096
DEPTH = 1
DEC_BATCH = 8
DEC_SEQ = 64
PAST_LEN = 2048

CHUNK = 64
Q_BLOCK = 128
MLA_HEADS = 4
QK_NOPE = 128
QK_ROPE = 64
V_HEAD = 128
Q_LORA = 768
KV_LORA = 256
ROPE_BASE = 10000.0
GMLP_HEADS = 4
GMLP_HEAD_DIM = 128
GMLP_CHUNK = 128
MLA_WIDTH = MLA_HEADS * V_HEAD
GMLP_WIDTH = GMLP_HEADS * GMLP_HEAD_DIM
D_MIX = MLA_WIDTH + GMLP_WIDTH
D_IN = Q_LORA + KV_LORA + QK_ROPE + 2 * GMLP_WIDTH
PEER_HEADS = 8
N_KEYS = 128
N_EXPERTS = N_KEYS * N_KEYS
PEER_TOPK = 16
D_KEY = 256
PEER_BLOCK = 128
DEEPNORM_ALPHA = (2 * DEPTH) ** 0.25
DEEPNORM_BETA = (8 * DEPTH) ** -0.25
EPS = 1e-6
NEG = -1e30

kernel_name = 'hymba_mla_gmlp_peer_deepnorm_stream_step'


def _layer_norm(x, g, b):
    xf = x.astype(jnp.float32)
    mu = jnp.mean(xf, axis=-1, keepdims=True)
    var = jnp.mean(jnp.square(xf - mu), axis=-1, keepdims=True)
    return ((xf - mu) * lax.rsqrt(var + EPS) * g.astype(jnp.float32) + b.astype(jnp.float32)).astype(x.dtype)


def _rms_norm(x, g):
    xf = x.astype(jnp.float32)
    ms = jnp.mean(jnp.square(xf), axis=-1, keepdims=True)
    return (xf * lax.rsqrt(ms + EPS) * g.astype(jnp.float32)).astype(x.dtype)


def _rope(x, pos):
    half = QK_ROPE // 2
    inv_freq = jnp.power(jnp.float32(ROPE_BASE), -jnp.arange(half, dtype=jnp.float32) * (2.0 / QK_ROPE))
    ang = pos.astype(jnp.float32)[:, None] * inv_freq[None, :]
    shape = (pos.shape[0],) + (1,) * (x.ndim - 3) + (half,)
    cos = jnp.cos(ang).reshape(shape)
    sin = jnp.sin(ang).reshape(shape)
    xf = x.astype(jnp.float32)
    x1, x2 = xf[..., :half], xf[..., half:]
    return jnp.concatenate([x1 * cos - x2 * sin, x2 * cos + x1 * sin], axis=-1).astype(x.dtype)


def _chunk_causal_attention(q, k, v, q_pos, k_pos):
    b, t, h, dq = q.shape
    scale = 1.0 / math.sqrt(dq)
    k_chunk = k_pos // CHUNK

    def attend(args):
        qb, qpb = args
        s = jnp.einsum('bqhd,bkhd->bhqk', qb, k).astype(jnp.float32) * scale
        mask = (qpb // CHUNK)[:, None] >= k_chunk[None, :]
        s = jnp.where(mask[None, None], s, NEG)
        p = jax.nn.softmax(s, axis=-1).astype(v.dtype)
        return jnp.einsum('bhqk,bkhd->bqhd', p, v)

    if t > Q_BLOCK and t % Q_BLOCK == 0:
        nb = t // Q_BLOCK
        qs = jnp.moveaxis(q.reshape(b, nb, Q_BLOCK, h, dq), 1, 0)
        ps = q_pos.reshape(nb, Q_BLOCK)
        o = lax.map(attend, (qs, ps))
        return jnp.moveaxis(o, 0, 1).reshape(b, t, h, v.shape[-1])
    return attend((q, q_pos))


def _gmlp_mixer(u, v, g_v, b_v, w_s, b_s):
    b, t, _ = u.shape
    u = jax.nn.gelu(u)
    v = _layer_norm(jax.nn.gelu(v), g_v, b_v)
    npad = (-t) % GMLP_CHUNK
    vp = jnp.pad(v, ((0, 0), (0, npad), (0, 0)))
    nc = (t + npad) // GMLP_CHUNK
    vp = vp.reshape(b, nc, GMLP_CHUNK, GMLP_HEADS, GMLP_HEAD_DIM)
    blk = jnp.arange(GMLP_CHUNK) // CHUNK
    mask = blk[:, None] >= blk[None, :]
    ws = jnp.where(mask[None], w_s, jnp.zeros_like(w_s))
    mixed = jnp.einsum('hij,bnjhd->bnihd', ws, vp) + jnp.transpose(b_s)[None, None, :, :, None]
    mixed = mixed.reshape(b, nc * GMLP_CHUNK, GMLP_WIDTH)[:, :t]
    return u * mixed, v


def _peer(h, w_pq, sub_keys, peer_u, peer_v):
    b, t, d = h.shape
    n = b * t
    npad = (-n) % PEER_BLOCK
    hf = jnp.pad(h.reshape(n, d), ((0, npad), (0, 0))).reshape(-1, PEER_BLOCK, d)

    def one_block(xb):
        q = (xb @ w_pq).reshape(PEER_BLOCK, PEER_HEADS, 2, D_KEY // 2)
        s = jnp.einsum('thpd,hpnd->thpn', q, sub_keys).astype(jnp.float32)
        s_top, i_top = lax.top_k(s, PEER_TOPK)
        cand = s_top[:, :, 0, :, None] + s_top[:, :, 1, None, :]
        cidx = i_top[:, :, 0, :, None] * N_KEYS + i_top[:, :, 1, None, :]
        cand = cand.reshape(PEER_BLOCK, PEER_HEADS, PEER_TOPK * PEER_TOPK)
        cidx = cidx.reshape(PEER_BLOCK, PEER_HEADS, PEER_TOPK * PEER_TOPK)
        f_s, f_pos = lax.top_k(cand, PEER_TOPK)
        e_idx = jnp.take_along_axis(cidx, f_pos, axis=-1)
        g = jax.nn.softmax(f_s, axis=-1).astype(xb.dtype)
        act = jax.nn.gelu(jnp.einsum('thkd,td->thk', peer_u[e_idx], xb))
        return jnp.einsum('thk,thkd->td', g * act, peer_v[e_idx])

    out = lax.map(one_block, hf).reshape(-1, d)[:n]
    return out.reshape(b, t, d)


def _layer(x, c, cache_ckv, cache_kr, w_ada, b_ada, w_in, g_q, g_kv, w_uq, w_uk, w_uv, g_v, b_v,
           w_s, b_s, g_out_a, g_out_b, w_o, ln1_g, ln1_b, w_pq, sub_keys, peer_u, peer_v, ln2_g, ln2_b):
    b, t, _ = x.shape
    past = 0 if cache_ckv is None else cache_ckv.shape[1]
    pos = past + jnp.arange(t, dtype=jnp.int32)
    mod = jax.nn.silu(c) @ w_ada + b_ada
    sh1, sc1, gt1, sh2, sc2, gt2 = jnp.split(mod[:, None, :], 6, axis=-1)
    h = x * (1 + sc1) + sh1
    z = h @ w_in
    o1 = Q_LORA
    o2 = o1 + KV_LORA
    o3 = o2 + QK_ROPE
    o4 = o3 + GMLP_WIDTH
    cq_raw, ckv_raw, kr_raw = z[..., :o1], z[..., o1:o2], z[..., o2:o3]
    gu, gv = z[..., o3:o4], z[..., o4:]
    cq = _rms_norm(cq_raw, g_q)
    q = (cq @ w_uq).reshape(b, t, MLA_HEADS, QK_NOPE + QK_ROPE)
    q = jnp.concatenate([q[..., :QK_NOPE], _rope(q[..., QK_NOPE:], pos)], axis=-1)
    ckv = _rms_norm(ckv_raw, g_kv)
    kr = _rope(kr_raw, pos)
    if cache_ckv is None:
        ckv_all, kr_all = ckv, kr
    else:
        ckv_all = jnp.concatenate([cache_ckv, ckv], axis=1)
        kr_all = jnp.concatenate([cache_kr, kr], axis=1)
    n_keys = ckv_all.shape[1]
    k_pos = jnp.arange(n_keys, dtype=jnp.int32)
    k_nope = (ckv_all @ w_uk).reshape(b, n_keys, MLA_HEADS, QK_NOPE)
    val = (ckv_all @ w_uv).reshape(b, n_keys, MLA_HEADS, V_HEAD)
    k = jnp.concatenate([k_nope, jnp.broadcast_to(kr_all[:, :, None, :], (b, n_keys, MLA_HEADS, QK_ROPE))], axis=-1)
    a_out = _chunk_causal_attention(q, k, val, pos, k_pos).reshape(b, t, MLA_WIDTH)
    b_out, v_rows = _gmlp_mixer(gu, gv, g_v, b_v, w_s, b_s)
    mix = jnp.concatenate([_rms_norm(a_out, g_out_a), _rms_norm(b_out, g_out_b)], axis=-1) @ w_o
    x1 = _layer_norm(DEEPNORM_ALPHA * x + (1 + gt1) * mix, ln1_g, ln1_b)
    h2 = x1 * (1 + sc2) + sh2
    f = _peer(h2, w_pq, sub_keys, peer_u, peer_v)
    x2 = _layer_norm(DEEPNORM_ALPHA * x1 + (1 + gt2) * f, ln2_g, ln2_b)
    return x2, ckv, kr, v_rows


def setup_inputs(seed: int = 0) -> dict:
    key = jax.random.key(seed)
    ks = iter(jax.random.split(key, 40))

    def nrm(shape, s=1.0):
        return jax.random.normal(next(ks), shape, jnp.float32) * s

    L = DEPTH
    D = D_MODEL
    return {
        'x_prompt': nrm((BATCH, SEQ, D)),
        'x_sample': nrm((DEC_BATCH, DEC_SEQ, D)),
        'cache_ckv': nrm((L, DEC_BATCH, PAST_LEN, KV_LORA)),
        'cache_krope': nrm((L, DEC_BATCH, PAST_LEN, QK_ROPE)),
        'c_prompt': nrm((BATCH, D)),
        'c_sample': nrm((DEC_BATCH, D)),
        'ln_in_g': 1.0 + nrm((D,), 0.05),
        'ln_in_b': nrm((D,), 0.02),
        'w_ada': nrm((L, D, 6 * D), 0.2 * D ** -0.5),
        'b_ada': nrm((L, 6 * D), 0.01),
        'w_in': nrm((L, D, D_IN), D ** -0.5),
        'g_q': 1.0 + nrm((L, Q_LORA), 0.05),
        'g_kv': 1.0 + nrm((L, KV_LORA), 0.05),
        'w_uq': nrm((L, Q_LORA, MLA_HEADS * (QK_NOPE + QK_ROPE)), Q_LORA ** -0.5),
        'w_uk': nrm((L, KV_LORA, MLA_HEADS * QK_NOPE), KV_LORA ** -0.5),
        'w_uv': nrm((L, KV_LORA, MLA_HEADS * V_HEAD), DEEPNORM_BETA * KV_LORA ** -0.5),
        'g_v': 1.0 + nrm((L, GMLP_WIDTH), 0.05),
        'b_v': nrm((L, GMLP_WIDTH), 0.02),
        'w_s': nrm((L, GMLP_HEADS, GMLP_CHUNK, GMLP_CHUNK), 0.5 * GMLP_CHUNK ** -0.5),
        'b_s': 1.0 + nrm((L, GMLP_HEADS, GMLP_CHUNK), 0.01),
        'g_out_a': 1.0 + nrm((L, MLA_WIDTH), 0.05),
        'g_out_b': 1.0 + nrm((L, GMLP_WIDTH), 0.05),
        'w_o': nrm((L, D_MIX, D), DEEPNORM_BETA * D_MIX ** -0.5),
        'ln1_g': 1.0 + nrm((L, D), 0.05),
        'ln1_b': nrm((L, D), 0.02),
        'w_pq': nrm((L, D, PEER_HEADS * D_KEY), D ** -0.5),
        'sub_keys': nrm((L, PEER_HEADS, 2, N_KEYS, D_KEY // 2), (D_KEY // 2) ** -0.5),
        'peer_u': nrm((L, N_EXPERTS, D), D ** -0.5),
        'peer_v': nrm((L, N_EXPERTS, D), DEEPNORM_BETA),
        'ln2_g': 1.0 + nrm((L, D), 0.05),
        'ln2_b': nrm((L, D), 0.02),
    }


def reference(x_prompt, x_sample, cache_ckv, cache_krope, c_prompt, c_sample, ln_in_g, ln_in_b,
              w_ada, b_ada, w_in, g_q, g_kv, w_uq, w_uk, w_uv, g_v, b_v, w_s, b_s, g_out_a, g_out_b,
              w_o, ln1_g, ln1_b, w_pq, sub_keys, peer_u, peer_v, ln2_g, ln2_b):
    hp = _layer_norm(x_prompt, ln_in_g, ln_in_b)
    hs = _layer_norm(x_sample, ln_in_g, ln_in_b)
    ckv_p, kr_p, ckv_s, kr_s, gv_s = [], [], [], [], []
    for l in range(DEPTH):
        lw = (w_ada[l], b_ada[l], w_in[l], g_q[l], g_kv[l], w_uq[l], w_uk[l], w_uv[l], g_v[l], b_v[l],
              w_s[l], b_s[l], g_out_a[l], g_out_b[l], w_o[l], ln1_g[l], ln1_b[l], w_pq[l], sub_keys[l],
              peer_u[l], peer_v[l], ln2_g[l], ln2_b[l])
        hp, ckv_new_p, kr_new_p, _ = _layer(hp, c_prompt, None, None, *lw)
        hs, ckv_new_s, kr_new_s, v_new_s = _layer(hs, c_sample, cache_ckv[l], cache_krope[l], *lw)
        ckv_p.append(ckv_new_p)
        kr_p.append(kr_new_p)
        ckv_s.append(ckv_new_s)
        kr_s.append(kr_new_s)
        gv_s.append(v_new_s)
    return (hp, hs, jnp.stack(ckv_p), jnp.stack(kr_p), jnp.stack(ckv_s), jnp.stack(kr_s), jnp.stack(gv_s))
```

```python
import functools
import math

import jax
import jax.numpy as jnp
from jax import lax
from jax.experimental import pallas as pl
from jax.experimental.pallas import tpu as pltpu

F32 = jnp.float32
BF16 = jnp.bfloat16

CHUNK = 64
MLA_HEADS = 4
QK_NOPE = 128
QK_ROPE = 64
V_HEAD = 128
Q_LORA = 768
KV_LORA = 256
ROPE_BASE = 10000.0
HEAD_PAD = 2 * QK_NOPE
GMLP_HEADS = 4
GMLP_HEAD_DIM = 128
GMLP_CHUNK = 128
MLA_WIDTH = MLA_HEADS * V_HEAD
GMLP_WIDTH = GMLP_HEADS * GMLP_HEAD_DIM
PEER_HEADS = 8
N_KEYS = 128
PEER_TOPK = 16
D_KEY = 256
EPS = 1e-6
NEG = -1e30
LOWEST = -3.0e38

LANES = 128
VMEM_LIMIT = 56 * 1024 * 1024

CAND_COUNTS = tuple(PEER_TOPK // (k + 1) for k in range(PEER_TOPK))
CAND_ROWS = 64


def _cparams(sem):
    return pltpu.CompilerParams(dimension_semantics=sem, vmem_limit_bytes=VMEM_LIMIT)


def _gelu(x):
    return 0.5 * x * (1.0 + jnp.tanh(math.sqrt(2.0 / math.pi) * (x + 0.044715 * (x * x * x))))


def _layer_norm(x, g, b):
    mu = jnp.mean(x, axis=-1, keepdims=True)
    xc = x - mu
    var = jnp.mean(xc * xc, axis=-1, keepdims=True)
    return xc * lax.rsqrt(var + EPS) * g + b


def _rms_norm(x, g):
    ms = jnp.mean(x * x, axis=-1, keepdims=True)
    return x * lax.rsqrt(ms + EPS) * g


def _dot(a, b):
    return jnp.dot(a, b, preferred_element_type=F32)


def _dot_nt(a, b):
    return lax.dot_general(a, b, (((1,), (1,)), ((), ())), preferred_element_type=F32)


def _ada_kernel(c_ref, w_ref, b_ref, o_ref):
    c = c_ref[...]
    s = c * (1.0 / (1.0 + jnp.exp(-c)))
    o_ref[...] = _dot(s.astype(BF16), w_ref[...].astype(BF16)) + b_ref[...]


def _ada_mod(c, w_ada, b_ada):
    bc, d = c.shape
    n = w_ada.shape[1]
    tn = 1536
    return pl.pallas_call(
        _ada_kernel,
        grid=(n // tn,),
        in_specs=[pl.BlockSpec((bc, d), lambda j: (0, 0)),
                  pl.BlockSpec((d, tn), lambda j: (0, j)),
                  pl.BlockSpec((1, tn), lambda j: (0, j))],
        out_specs=pl.BlockSpec((bc, tn), lambda j: (0, j)),
        out_shape=jax.ShapeDtypeStruct((bc, n), F32),
        compiler_params=_cparams(("arbitrary",)),
        name="ada_mod",
    )(c, w_ada, b_ada.reshape(1, n))


def _proj_kernel(x_ref, lng_ref, lnb_ref, mod_ref, win_ref, gq_ref, gkv_ref, wuq_ref, wukv_ref,
                 gv_ref, bv_ref, ws_ref, bs_ref, gob_ref, cos_ref, sin_ref,
                 q_ref, k_ref, v_ref, ckv_ref, kr_ref, bn_ref, *vrow_refs, d_model):
    d = d_model
    tm = x_ref.shape[0]
    h0 = _layer_norm(x_ref[...], lng_ref[...], lnb_ref[...])
    sh1 = mod_ref[:, 0:d]
    sc1 = mod_ref[:, d:2 * d]
    h = h0 * (1.0 + sc1) + sh1
    z = _dot(h.astype(BF16), win_ref[...])
    o1 = Q_LORA
    o2 = o1 + KV_LORA
    o3 = o2 + GMLP_WIDTH
    o4 = o3 + GMLP_WIDTH
    o5 = o4 + LANES
    cos = cos_ref[...]
    sin = sin_ref[...]

    cq = _rms_norm(z[:, 0:o1], gq_ref[...])
    qa = _dot(cq.astype(BF16), wuq_ref[...])
    nw = MLA_HEADS * QK_NOPE
    rw = MLA_HEADS * QK_ROPE
    qr = qa[:, nw:nw + rw] * cos + qa[:, nw + rw:nw + 2 * rw] * sin
    lane = lax.broadcasted_iota(jnp.int32, (tm, LANES), 1)
    for hd in range(MLA_HEADS):
        base = hd * HEAD_PAD
        q_ref[:, base:base + QK_NOPE] = qa[:, hd * QK_NOPE:(hd + 1) * QK_NOPE].astype(BF16)
        tile = qr[:, (hd // 2) * LANES:(hd // 2 + 1) * LANES]
        keep = (lane < QK_ROPE) if hd % 2 == 0 else (lane >= QK_ROPE)
        q_ref[:, base + QK_NOPE:base + HEAD_PAD] = jnp.where(keep, tile, 0.0).astype(BF16)

    ckv = _rms_norm(z[:, o1:o2], gkv_ref[...])
    ckv_ref[...] = ckv
    kv = _dot(ckv.astype(BF16), wukv_ref[...])
    krd = z[:, o4:o5] * cos[:, 0:LANES] + z[:, o5:o5 + LANES] * sin[:, 0:LANES]
    kr_ref[...] = krd[:, 0:QK_ROPE]
    krd_b = krd.astype(BF16)
    for hd in range(MLA_HEADS):
        base = hd * HEAD_PAD
        k_ref[:, base:base + QK_NOPE] = kv[:, hd * QK_NOPE:(hd + 1) * QK_NOPE].astype(BF16)
        k_ref[:, base + QK_NOPE:base + HEAD_PAD] = krd_b
    v_ref[...] = kv[:, nw:nw + MLA_WIDTH].astype(BF16)

    u = _gelu(z[:, o2:o3])
    v = _layer_norm(_gelu(z[:, o3:o4]), gv_ref[...], bv_ref[...])
    if vrow_refs:
        vrow_refs[0][...] = v
    vb = v.astype(BF16)
    cr = min(tm, GMLP_CHUNK)
    ri = lax.broadcasted_iota(jnp.int32, (cr, cr), 0) // CHUNK
    ci = lax.broadcasted_iota(jnp.int32, (cr, cr), 1) // CHUNK
    chunks = []
    for c in range(tm // cr):
        heads = []
        for hd in range(GMLP_HEADS):
            w = jnp.where(ri >= ci, ws_ref[hd, 0:cr, 0:cr], 0.0).astype(BF16)
            cols = slice(hd * GMLP_HEAD_DIM, (hd + 1) * GMLP_HEAD_DIM)
            mixed = _dot(w, vb[c * cr:(c + 1) * cr, cols]) + bs_ref[hd, 0:cr, :]
            heads.append(u[c * cr:(c + 1) * cr, cols] * mixed)
        chunks.append(jnp.concatenate(heads, axis=1))
    b_out = chunks[0] if len(chunks) == 1 else jnp.concatenate(chunks, axis=0)
    bn_ref[...] = _rms_norm(b_out, gob_ref[...]).astype(BF16)


def _input_projection(x, ln_g, ln_b, mod3, w, cos, sin, tm, with_vrows):
    b, t, d = x.shape
    assert t % tm == 0 and (tm % GMLP_CHUNK == 0 or (tm == t and t < GMLP_CHUNK))
    full = lambda a: pl.BlockSpec(a.shape, lambda bi, i: (0,) * a.ndim)
    tok = lambda width: pl.BlockSpec((None, tm, width), lambda bi, i: (bi, i, 0))
    weights = [w["w_in"], w["g_q"], w["g_kv"], w["w_uq"], w["w_ukv"], w["g_v"], w["b_v"], w["w_s"], w["b_s"],
               w["g_out_b"]]
    out_widths = [(MLA_HEADS * HEAD_PAD, BF16), (MLA_HEADS * HEAD_PAD, BF16), (MLA_WIDTH, BF16),
                  (KV_LORA, F32), (QK_ROPE, F32), (GMLP_WIDTH, BF16)]
    if with_vrows:
        out_widths.append((GMLP_WIDTH, F32))
    return pl.pallas_call(
        functools.partial(_proj_kernel, d_model=d),
        grid=(b, t // tm),
        in_specs=[tok(d), full(ln_g), full(ln_b),
                  pl.BlockSpec((None, 1, mod3.shape[2]), lambda bi, i: (bi, 0, 0))]
                 + [full(a) for a in weights]
                 + [pl.BlockSpec((tm, cos.shape[1]), lambda bi, i: (i, 0))] * 2,
        out_specs=[tok(wd) for wd, _ in out_widths],
        out_shape=[jax.ShapeDtypeStruct((b, t, wd), dt) for wd, dt in out_widths],
        compiler_params=_cparams(("parallel", "parallel")),
        name="input_projection",
    )(x, ln_g, ln_b, mod3, *weights, cos, sin)


def _cache_kernel(ckv_ref, kr_ref, wukv_ref, dup_ref, k_ref, v_ref):
    kv = _dot(ckv_ref[...].astype(BF16), wukv_ref[...])
    krd = _dot(kr_ref[...].astype(BF16), dup_ref[...]).astype(BF16)
    nw = MLA_HEADS * QK_NOPE
    for hd in range(MLA_HEADS):
        base = hd * HEAD_PAD
        k_ref[:, base:base + QK_NOPE] = kv[:, hd * QK_NOPE:(hd + 1) * QK_NOPE].astype(BF16)
        k_ref[:, base + QK_NOPE:base + HEAD_PAD] = krd
    v_ref[...] = kv[:, nw:nw + MLA_WIDTH].astype(BF16)


def _cache_projection(cache_ckv, cache_kr, w_ukv, dup, tm):
    b, p, _ = cache_ckv.shape
    assert p % tm == 0
    tok = lambda width: pl.BlockSpec((None, tm, width), lambda bi, i: (bi, i, 0))
    full = lambda a: pl.BlockSpec(a.shape, lambda bi, i: (0,) * a.ndim)
    return pl.pallas_call(
        _cache_kernel,
        grid=(b, p // tm),
        in_specs=[tok(KV_LORA), tok(QK_ROPE), full(w_ukv), full(dup)],
        out_specs=[tok(MLA_HEADS * HEAD_PAD), tok(MLA_WIDTH)],
        out_shape=[jax.ShapeDtypeStruct((b, p, MLA_HEADS * HEAD_PAD), BF16),
                   jax.ShapeDtypeStruct((b, p, MLA_WIDTH), BF16)],
        compiler_params=_cparams(("parallel", "parallel")),
        name="cache_projection",
    )(cache_ckv, cache_kr, w_ukv, dup)


def _attn_kernel(q_ref, k_ref, v_ref, ga_ref, o_ref, *, past, n_valid, tq, tk):
    i = pl.program_id(1)
    q_first = past + i * tq
    hi = jnp.minimum(((q_first + tq - 1) // CHUNK + 1) * CHUNK, n_valid)
    n_tiles = (hi + tk - 1) // tk
    scale = 1.0 / math.sqrt(QK_NOPE + QK_ROPE)
    q_chunk = (q_first + lax.broadcasted_iota(jnp.int32, (tq, tk), 0)) // CHUNK
    k_iota = lax.broadcasted_iota(jnp.int32, (tq, tk), 1)
    outs = []
    for hd in range(MLA_HEADS):
        qh = q_ref[:, hd * HEAD_PAD:(hd + 1) * HEAD_PAD]

        def step(j, carry, hd=hd, qh=qh):
            m, l, acc = carry
            off = pl.multiple_of(j * tk, tk)
            kh = k_ref[pl.ds(off, tk), hd * HEAD_PAD:(hd + 1) * HEAD_PAD]
            vh = v_ref[pl.ds(off, tk), hd * V_HEAD:(hd + 1) * V_HEAD]
            s = _dot_nt(qh, kh) * scale
            k_pos = off + k_iota
            s = jnp.where((k_pos // CHUNK <= q_chunk) & (k_pos < n_valid), s, NEG)
            m_new = jnp.maximum(m, jnp.max(s, axis=-1, keepdims=True))
            alpha = jnp.exp(m - m_new)
            p = jnp.exp(s - m_new)
            l = alpha * l + jnp.sum(p, axis=-1, keepdims=True)
            acc = alpha * acc + _dot(p.astype(BF16), vh)
            return m_new, l, acc

        init = (jnp.full((tq, 1), NEG, F32), jnp.zeros((tq, 1), F32), jnp.zeros((tq, V_HEAD), F32))
        _, l, acc = lax.fori_loop(0, n_tiles, step, init)
        outs.append(acc / l)
    a = jnp.concatenate(outs, axis=1)
    o_ref[...] = _rms_norm(a, ga_ref[...]).astype(BF16)


def _attention(q, k, v, g_out_a, past, n_valid, tq, tk):
    b, t, _ = q.shape
    lk = k.shape[1]
    assert t % tq == 0 and lk % tk == 0 and n_valid <= lk
    return pl.pallas_call(
        functools.partial(_attn_kernel, past=past, n_valid=n_valid, tq=tq, tk=tk),
        grid=(b, t // tq),
        in_specs=[pl.BlockSpec((None, tq, q.shape[2]), lambda bi, i: (bi, i, 0)),
                  pl.BlockSpec((None, lk, k.shape[2]), lambda bi, i: (bi, 0, 0)),
                  pl.BlockSpec((None, lk, v.shape[2]), lambda bi, i: (bi, 0, 0)),
                  pl.BlockSpec(g_out_a.shape, lambda bi, i: (0, 0))],
        out_specs=pl.BlockSpec((None, tq, MLA_WIDTH), lambda bi, i: (bi, i, 0)),
        out_shape=jax.ShapeDtypeStruct((b, t, MLA_WIDTH), BF16),
        compiler_params=_cparams(("parallel", "arbitrary")),
        name="attention",
    )(q, k, v, g_out_a)


def _extract_top(s, rounds):
    r_, l_ = s.shape
    ridx = lax.broadcasted_iota(jnp.int32, (r_, l_), 0).astype(F32)
    vidx = lax.broadcasted_iota(jnp.int32, (rounds, l_), 0)
    rank = jnp.full((r_, l_), float(rounds), F32)
    vals = jnp.zeros((rounds, l_), F32)
    for r in range(rounds):
        m = jnp.max(s, axis=0, keepdims=True)
        first = jnp.min(jnp.where(s == m, ridx, float(r_)), axis=0, keepdims=True)
        hit = ridx == first
        rank = jnp.where(hit, float(r), rank)
        s = jnp.where(hit, LOWEST, s)
        vals = jnp.where(vidx == r, m, vals)
    return rank, vals


def _route_kernel(x_ref, an_ref, bn_ref, lng_ref, lnb_ref, mod_ref, wo_ref, l1g_ref, l1b_ref, wpq_ref, sk_ref,
                  grp_ref, x1_ref, h2t_ref, rank2_ref, e2_ref, nrow_ref, e1z_ref, cand_ref,
                  *, d_model, alpha, seg):
    d = d_model
    tm = x_ref.shape[0]
    h0 = _layer_norm(x_ref[...], lng_ref[...], lnb_ref[...])
    mix = _dot(an_ref[...], wo_ref[0:MLA_WIDTH, :]) + _dot(bn_ref[...], wo_ref[MLA_WIDTH:, :])
    parts = []
    for sgm in range(tm // seg):
        rows = slice(sgm * seg, (sgm + 1) * seg)
        gt1 = mod_ref[sgm, :, 2 * d:3 * d]
        sh2 = mod_ref[sgm, :, 3 * d:4 * d]
        sc2 = mod_ref[sgm, :, 4 * d:5 * d]
        x1 = _layer_norm(alpha * h0[rows] + (1.0 + gt1) * mix[rows], l1g_ref[...], l1b_ref[...])
        x1_ref[rows, :] = x1
        parts.append(x1 * (1.0 + sc2) + sh2)
    h2 = parts[0] if len(parts) == 1 else jnp.concatenate(parts, axis=0)
    h2t = h2.T.astype(BF16)
    h2t_ref[...] = h2t
    qpt = _dot(wpq_ref[...], h2t).astype(BF16)
    half = D_KEY // 2

    pad0 = sum(CAND_COUNTS) // 8 * 8
    cand_ref[pad0:CAND_ROWS, :] = jnp.full((CAND_ROWS - pad0, tm), LOWEST, F32)
    for hd in range(PEER_HEADS):
        base = hd * D_KEY
        s1 = _dot(sk_ref[2 * hd], qpt[base:base + half, :])
        s2 = _dot(sk_ref[2 * hd + 1], qpt[base + half:base + D_KEY, :])
        rank1, a = _extract_top(s1, PEER_TOPK)
        rank2, bvals = _extract_top(s2, PEER_TOPK)
        off = 0
        for k1, cnt in enumerate(CAND_COUNTS):
            cand_ref[off:off + cnt, :] = a[k1:k1 + 1, :] + bvals[0:cnt, :]
            off += cnt
        cand = cand_ref[...]
        crank, _ = _extract_top(cand, PEER_TOPK)
        sel = crank < float(PEER_TOPK)
        z = jnp.sum(jnp.where(sel, jnp.exp(cand - cand[0:1, :]), 0.0), axis=0, keepdims=True)
        n = _dot(grp_ref[...], jnp.where(sel, 1.0, 0.0).astype(BF16))
        nrow = jnp.zeros((N_KEYS, tm), F32)
        for k1 in range(PEER_TOPK):
            nrow = jnp.where(rank1 == float(k1), n[k1:k1 + 1, :], nrow)
        rank2_ref[hd] = rank2.astype(BF16)
        e2_ref[hd] = jnp.exp(s2 - bvals[0:1, :]).astype(BF16)
        nrow_ref[hd] = nrow
        e1z_ref[hd] = jnp.exp(s1 - a[0:1, :]) / z


def _route(x2d, an2d, bn2d, ln_g, ln_b, mod3, w, tm, seg, t_batch, alpha):
    n, d = x2d.shape
    assert n % tm == 0 and tm % seg == 0 and tm % LANES == 0
    nseg = tm // seg
    tok = lambda width: pl.BlockSpec((tm, width), lambda i: (i, 0))
    full = lambda a: pl.BlockSpec(a.shape, lambda i: (0,) * a.ndim)
    tbl = pl.BlockSpec((PEER_HEADS, N_KEYS, tm), lambda i: (0, 0, i))
    weights = [w["w_o"], w["ln1_g"], w["ln1_b"], w["w_pq_t"], w["sub_keys"], w["cand_groups"]]
    return pl.pallas_call(
        functools.partial(_route_kernel, d_model=d, alpha=alpha, seg=seg),
        grid=(n // tm,),
        in_specs=[tok(d), tok(MLA_WIDTH), tok(GMLP_WIDTH), full(ln_g), full(ln_b),
                  pl.BlockSpec((nseg, 1, mod3.shape[2]), lambda i: ((i * tm // t_batch) // nseg, 0, 0))]
                 + [full(a) for a in weights],
        out_specs=[tok(d), pl.BlockSpec((d, tm), lambda i: (0, i)), tbl, tbl, tbl, tbl],
        out_shape=[jax.ShapeDtypeStruct((n, d), F32), jax.ShapeDtypeStruct((d, n), BF16),
                   jax.ShapeDtypeStruct((PEER_HEADS, N_KEYS, n), BF16),
                   jax.ShapeDtypeStruct((PEER_HEADS, N_KEYS, n), BF16),
                   jax.ShapeDtypeStruct((PEER_HEADS, N_KEYS, n), F32),
                   jax.ShapeDtypeStruct((PEER_HEADS, N_KEYS, n), F32)],
        scratch_shapes=[pltpu.VMEM((CAND_ROWS, tm), F32)],
        compiler_params=_cparams(("parallel",)),
        name="route",
    )(x2d, an2d, bn2d, ln_g, ln_b, mod3, *weights)


def _peer_kernel(h2t_ref, rank2_ref, e2_ref, nrow_ref, e1z_ref, u_ref, vt_ref, o_ref, a_ref, *, tw):
    c = pl.program_id(1)
    tb = h2t_ref.shape[1]
    rows_per_chunk = u_ref.shape[0] // N_KEYS

    @pl.when(c == 0)
    def _():
        o_ref[...] = jnp.zeros_like(o_ref)

    def subtile(ts, carry):
        off = pl.multiple_of(ts * tw, tw)
        ht = h2t_ref[:, pl.ds(off, tw)]
        for p in range(rows_per_chunk):
            s = _dot(u_ref[p * N_KEYS:(p + 1) * N_KEYS, :], ht)
            wgt = jnp.zeros((N_KEYS, tw), BF16)
            for hd in range(PEER_HEADS):
                nr = nrow_ref[hd, p:p + 1, pl.ds(off, tw)].astype(BF16)
                e1 = e1z_ref[hd, p:p + 1, pl.ds(off, tw)].astype(BF16)
                r2 = rank2_ref[hd, :, pl.ds(off, tw)]
                e2 = e2_ref[hd, :, pl.ds(off, tw)]
                wgt = wgt + jnp.where(r2 < nr, e2, jnp.zeros_like(e2)) * e1
            a_ref[p * N_KEYS:(p + 1) * N_KEYS, :] = (_gelu(s) * wgt.astype(F32)).astype(BF16)
        o_ref[:, pl.ds(off, tw)] += _dot(vt_ref[...], a_ref[...])
        return carry

    lax.fori_loop(0, tb // tw, subtile, 0)


def _peer_mix(h2t, rank2, e2, nrow, e1z, u_b, vt_b, tb, tw, ec):
    d, n = h2t.shape
    ne = u_b.shape[0]
    assert n % tb == 0 and tb % tw == 0 and ne % ec == 0 and ec % (8 * N_KEYS) == 0
    rows = ec // N_KEYS
    tbl = pl.BlockSpec((PEER_HEADS, N_KEYS, tb), lambda b, c: (0, 0, b))
    row = pl.BlockSpec((PEER_HEADS, rows, tb), lambda b, c: (0, c, b))
    return pl.pallas_call(
        functools.partial(_peer_kernel, tw=tw),
        grid=(n // tb, ne // ec),
        in_specs=[pl.BlockSpec((d, tb), lambda b, c: (0, b)), tbl, tbl, row, row,
                  pl.BlockSpec((ec, d), lambda b, c: (c, 0)),
                  pl.BlockSpec((d, ec), lambda b, c: (0, c))],
        out_specs=pl.BlockSpec((d, tb), lambda b, c: (0, b)),
        out_shape=jax.ShapeDtypeStruct((d, n), F32),
        scratch_shapes=[pltpu.VMEM((ec, tw), BF16)],
        compiler_params=_cparams(("parallel", "arbitrary")),
        name="peer_mix",
    )(h2t, rank2, e2, nrow, e1z, u_b, vt_b)


def _final_kernel(ft_ref, x1_ref, mod_ref, g_ref, b_ref, o_ref, *, d_model, alpha, seg):
    d = d_model
    tm = x1_ref.shape[0]
    f = ft_ref[...].T
    for sgm in range(tm // seg):
        rows = slice(sgm * seg, (sgm + 1) * seg)
        gt2 = mod_ref[sgm, :, 5 * d:6 * d]
        o_ref[rows, :] = _layer_norm(alpha * x1_ref[rows, :] + (1.0 + gt2) * f[rows], g_ref[...], b_ref[...])


def _final(ft, x1, mod3, g, b, tm, seg, t_batch, alpha):
    n, d = x1.shape
    nseg = tm // seg
    return pl.pallas_call(
        functools.partial(_final_kernel, d_model=d, alpha=alpha, seg=seg),
        grid=(n // tm,),
        in_specs=[pl.BlockSpec((d, tm), lambda i: (0, i)), pl.BlockSpec((tm, d), lambda i: (i, 0)),
                  pl.BlockSpec((nseg, 1, mod3.shape[2]), lambda i: ((i * tm // t_batch) // nseg, 0, 0)),
                  pl.BlockSpec(g.shape, lambda i: (0, 0)), pl.BlockSpec(b.shape, lambda i: (0, 0))],
        out_specs=pl.BlockSpec((tm, d), lambda i: (i, 0)),
        out_shape=jax.ShapeDtypeStruct((n, d), F32),
        compiler_params=_cparams(("parallel",)),
        name="final_norm",
    )(ft, x1, mod3, g, b)


def _rope_tables(past, t):
    half = QK_ROPE // 2
    inv_freq = jnp.power(jnp.float32(ROPE_BASE), -jnp.arange(half, dtype=F32) * (2.0 / QK_ROPE))
    pos = past + jnp.arange(t, dtype=jnp.int32)
    ang = pos.astype(F32)[:, None] * inv_freq[None, :]
    cos = jnp.cos(ang)
    sin = jnp.sin(ang)
    cos_t = jnp.tile(jnp.concatenate([cos, cos], axis=1), (1, MLA_HEADS))
    sin_t = jnp.tile(jnp.concatenate([-sin, sin], axis=1), (1, MLA_HEADS))
    return cos_t, sin_t


def _swap_halves(w):
    half = QK_ROPE // 2
    return jnp.concatenate([w[..., half:], w[..., :half]], axis=-1)


def _prepare_layer(w_in, g_q, g_kv, w_uq, w_uk, w_uv, g_v, b_v, w_s, b_s, g_out_a, g_out_b, w_o,
                   ln1_g, ln1_b, w_pq, sub_keys, peer_u, peer_v, ln2_g, ln2_b):
    o1 = Q_LORA
    o2 = o1 + KV_LORA
    o3 = o2 + QK_ROPE
    o4 = o3 + GMLP_WIDTH
    kr_w = w_in[:, o2:o3]
    kr_sw = _swap_halves(kr_w)
    w_in_x = jnp.concatenate([w_in[:, :o2], w_in[:, o3:o4], w_in[:, o4:], kr_w, kr_w, kr_sw, kr_sw], axis=1)
    uq = w_uq.reshape(Q_LORA, MLA_HEADS, QK_NOPE + QK_ROPE)
    uq_rope = uq[:, :, QK_NOPE:]
    w_uq_x = jnp.concatenate([uq[:, :, :QK_NOPE].reshape(Q_LORA, -1), uq_rope.reshape(Q_LORA, -1),
                              _swap_halves(uq_rope).reshape(Q_LORA, -1)], axis=1)
    row = lambda a: a.reshape(1, -1)
    groups = []
    for k1, cnt in enumerate(CAND_COUNTS):
        groups += [k1] * cnt
    groups += [-1] * (CAND_ROWS - len(groups))
    cand_groups = (jnp.arange(PEER_TOPK)[:, None] == jnp.asarray(groups)[None, :]).astype(BF16)
    return dict(
        w_in=w_in_x.astype(BF16), g_q=row(g_q), g_kv=row(g_kv), w_uq=w_uq_x.astype(BF16),
        w_ukv=jnp.concatenate([w_uk, w_uv], axis=1).astype(BF16), g_v=row(g_v), b_v=row(b_v), w_s=w_s,
        b_s=jnp.broadcast_to(b_s[:, :, None], b_s.shape + (GMLP_HEAD_DIM,)), g_out_a=row(g_out_a),
        g_out_b=row(g_out_b), w_o=w_o.astype(BF16), ln1_g=row(ln1_g), ln1_b=row(ln1_b),
        w_pq_t=w_pq.T.astype(BF16), sub_keys=sub_keys.reshape(PEER_HEADS * 2, N_KEYS, D_KEY // 2).astype(BF16),
        cand_groups=cand_groups, peer_u=peer_u.astype(BF16), peer_vt=peer_v.T.astype(BF16),
        ln2_g=row(ln2_g), ln2_b=row(ln2_b),
        kr_dup=jnp.concatenate([jnp.eye(QK_ROPE, dtype=BF16)] * 2, axis=1),
    )


def _pick_tile(n, prefs):
    for p in prefs:
        if n % p == 0:
            return p
    raise ValueError(f"no tile for {n}")


def _layer_group(x, mod, cache_ckv, cache_kr, w, ln_g, ln_b, alpha, normalize_input):
    del normalize_input
    b, t, d = x.shape
    n = b * t
    past = 0 if cache_ckv is None else cache_ckv.shape[1]
    mod3 = mod.reshape(b, 1, mod.shape[1])
    cos, sin = _rope_tables(past, t)

    tm = _pick_tile(t, (256, 128)) if t >= GMLP_CHUNK else t
    outs = _input_projection(x, ln_g, ln_b, mod3, w, cos, sin, tm, cache_ckv is not None)
    q, k_new, v_new, ckv, kr, bn = outs[:6]
    v_rows = outs[6] if cache_ckv is not None else None

    tq = _pick_tile(t, (256, 128, 64))
    tk = 256
    if cache_ckv is None:
        assert t % tk == 0
        k_all, v_all, n_valid = k_new, v_new, t
    else:
        k_c, v_c = _cache_projection(cache_ckv, cache_kr, w["w_ukv"], w["kr_dup"], _pick_tile(past, (512, 256, 128)))
        n_valid = past + t
        pad = (-n_valid) % tk
        k_all = jnp.concatenate([k_c, k_new, jnp.zeros((b, pad, k_new.shape[2]), BF16)], axis=1)
        v_all = jnp.concatenate([v_c, v_new, jnp.zeros((b, pad, v_new.shape[2]), BF16)], axis=1)
    an = _attention(q, k_all, v_all, w["g_out_a"], past, n_valid, tq, tk)

    tr = _pick_tile(n, (256, 128))
    seg = min(t, tr)
    assert tr % seg == 0 and t % seg == 0
    x1, h2t, rank2, e2, nrow, e1z = _route(x.reshape(n, d), an.reshape(n, -1), bn.reshape(n, -1), ln_g, ln_b,
                                          mod3, w, tr, seg, t, alpha)
    tb = _pick_tile(n, (512, 256))
    ft = _peer_mix(h2t, rank2, e2, nrow, e1z, w["peer_u"], w["peer_vt"], tb, 256, 8 * N_KEYS)
    y = _final(ft, x1, mod3, w["ln2_g"], w["ln2_b"], tr, seg, t, alpha)
    return y.reshape(b, t, d), ckv, kr, v_rows


def kernel(x_prompt, x_sample, cache_ckv, cache_krope, c_prompt, c_sample, ln_in_g, ln_in_b, w_ada, b_ada, w_in,
           g_q, g_kv, w_uq, w_uk, w_uv, g_v, b_v, w_s, b_s, g_out_a, g_out_b, w_o, ln1_g, ln1_b, w_pq, sub_keys,
           peer_u, peer_v, ln2_g, ln2_b):
    depth = w_ada.shape[0]
    assert depth == 1, "the entry LayerNorm is fused into the first layer's kernels"
    alpha = (2 * depth) ** 0.25
    bp = c_prompt.shape[0]
    bs = c_sample.shape[0]
    c_all = jnp.concatenate([c_prompt, c_sample], axis=0)
    c_all = jnp.pad(c_all, ((0, (-c_all.shape[0]) % 8), (0, 0)))
    ln_g = ln_in_g.reshape(1, -1)
    ln_b = ln_in_b.reshape(1, -1)
    hp, hs = x_prompt, x_sample
    ckv_p, kr_p, ckv_s, kr_s, gv_s = [], [], [], [], []
    for l in range(depth):
        w = _prepare_layer(w_in[l], g_q[l], g_kv[l], w_uq[l], w_uk[l], w_uv[l], g_v[l], b_v[l], w_s[l], b_s[l],
                           g_out_a[l], g_out_b[l], w_o[l], ln1_g[l], ln1_b[l], w_pq[l], sub_keys[l], peer_u[l],
                           peer_v[l], ln2_g[l], ln2_b[l])
        mod = _ada_mod(c_all, w_ada[l], b_ada[l])
        hp, ckv_new_p, kr_new_p, _ = _layer_group(hp, mod[:bp], None, None, w, ln_g, ln_b, alpha, l == 0)
        hs, ckv_new_s, kr_new_s, v_new_s = _layer_group(hs, mod[bp:bp + bs], cache_ckv[l], cache_krope[l], w,
                                                       ln_g, ln_b, alpha, l == 0)
        ckv_p.append(ckv_new_p)
        kr_p.append(kr_new_p)
        ckv_s.append(ckv_new_s)
        kr_s.append(kr_new_s)
        gv_s.append(v_new_s)
    return (hp, hs, jnp.stack(ckv_p), jnp.stack(kr_p), jnp.stack(ckv_s), jnp.stack(kr_s), jnp.stack(gv_s))
```

```python
import functools
import math

import jax
import jax.numpy as jnp
from jax import lax
from jax.experimental import pallas as pl
from jax.experimental.pallas import tpu as pltpu

F32 = jnp.float32
BF16 = jnp.bfloat16

CHUNK = 64
MLA_HEADS = 4
QK_NOPE = 128
QK_ROPE = 64
V_HEAD = 128
Q_LORA = 768
KV_LORA = 256
ROPE_BASE = 10000.0
HEAD_PAD = 2 * QK_NOPE
GMLP_HEADS = 4
GMLP_HEAD_DIM = 128
GMLP_CHUNK = 128
MLA_WIDTH = MLA_HEADS * V_HEAD
GMLP_WIDTH = GMLP_HEADS * GMLP_HEAD_DIM
PEER_HEADS = 8
N_KEYS = 128
PEER_TOPK = 16
D_KEY = 256
EPS = 1e-6
NEG = -1e30
LOWEST = -3.0e38

LANES = 128
BF16_ROWS = 16
VMEM_LIMIT = 56 * 1024 * 1024

CAND_COUNTS = tuple(PEER_TOPK // (k + 1) for k in range(PEER_TOPK))
CAND_ROWS = 64


def _cparams(sem):
    return pltpu.CompilerParams(dimension_semantics=sem, vmem_limit_bytes=VMEM_LIMIT)


def _gelu(x):
    c1 = -2.0 * math.sqrt(2.0 / math.pi) * math.log2(math.e)
    return x / (1.0 + jnp.exp2(x * (c1 + (c1 * 0.044715) * (x * x))))


def _layer_norm(x, g, b):
    mu = jnp.mean(x, axis=-1, keepdims=True)
    xc = x - mu
    var = jnp.mean(xc * xc, axis=-1, keepdims=True)
    return xc * lax.rsqrt(var + EPS) * g + b


def _rms_norm(x, g):
    ms = jnp.mean(x * x, axis=-1, keepdims=True)
    return x * lax.rsqrt(ms + EPS) * g


def _dot(a, b):
    return jnp.dot(a, b, preferred_element_type=F32)


def _transpose(x):
    return x.T


def _dot_nt(a, b):
    return lax.dot_general(a, b, (((1,), (1,)), ((), ())), preferred_element_type=F32)


def _ada_kernel(c_ref, w_ref, b_ref, o_ref):
    c = c_ref[...]
    s = c * (1.0 / (1.0 + jnp.exp(-c)))
    o_ref[...] = _dot(s.astype(BF16), w_ref[...].astype(BF16)) + b_ref[...]


def _ada_mod(c, w_ada, b_ada):
    bc, d = c.shape
    n = w_ada.shape[1]
    tn = 1536
    return pl.pallas_call(
        _ada_kernel,
        grid=(n // tn,),
        in_specs=[pl.BlockSpec((bc, d), lambda j: (0, 0)),
                  pl.BlockSpec((d, tn), lambda j: (0, j)),
                  pl.BlockSpec((1, tn), lambda j: (0, j))],
        out_specs=pl.BlockSpec((bc, tn), lambda j: (0, j)),
        out_shape=jax.ShapeDtypeStruct((bc, n), F32),
        compiler_params=_cparams(("arbitrary",)),
        name="ada_mod",
    )(c, w_ada, b_ada.reshape(1, n))


def _proj_kernel(x_ref, lng_ref, lnb_ref, mod_ref, win_ref, gq_ref, gkv_ref, wuq_ref, wukv_ref,
                 gv_ref, bv_ref, ws_ref, bs_ref, gob_ref, cos_ref, sin_ref,
                 q_ref, k_ref, vt_ref, ckv_ref, kr_ref, bn_ref, *vrow_refs, d_model):
    d = d_model
    tm = x_ref.shape[0]
    h0 = _layer_norm(x_ref[...], lng_ref[...], lnb_ref[...])
    sh1 = mod_ref[:, 0:d]
    sc1 = mod_ref[:, d:2 * d]
    h = h0 * (1.0 + sc1) + sh1
    z = _dot(h.astype(BF16), win_ref[...])
    o1 = Q_LORA
    o2 = o1 + KV_LORA
    o3 = o2 + GMLP_WIDTH
    o4 = o3 + GMLP_WIDTH
    o5 = o4 + LANES
    cos = cos_ref[...]
    sin = sin_ref[...]

    cq = _rms_norm(z[:, 0:o1], gq_ref[...])
    qa = _dot(cq.astype(BF16), wuq_ref[...])
    nw = MLA_HEADS * QK_NOPE
    rw = MLA_HEADS * QK_ROPE
    scale = math.log2(math.e) / math.sqrt(QK_NOPE + QK_ROPE)
    qr = (qa[:, nw:nw + rw] * cos + qa[:, nw + rw:nw + 2 * rw] * sin) * scale
    lane = lax.broadcasted_iota(jnp.int32, (tm, LANES), 1)
    for hd in range(MLA_HEADS):
        base = hd * HEAD_PAD
        q_ref[:, base:base + QK_NOPE] = (qa[:, hd * QK_NOPE:(hd + 1) * QK_NOPE] * scale).astype(BF16)
        tile = qr[:, (hd // 2) * LANES:(hd // 2 + 1) * LANES]
        keep = (lane < QK_ROPE) if hd % 2 == 0 else (lane >= QK_ROPE)
        q_ref[:, base + QK_NOPE:base + HEAD_PAD] = jnp.where(keep, tile, 0.0).astype(BF16)

    ckv = _rms_norm(z[:, o1:o2], gkv_ref[...])
    ckv_ref[...] = ckv
    kv = _dot(ckv.astype(BF16), wukv_ref[...])
    krd = z[:, o4:o5] * cos[:, 0:LANES] + z[:, o5:o5 + LANES] * sin[:, 0:LANES]
    kr_ref[...] = krd[:, 0:QK_ROPE]
    krd_b = krd.astype(BF16)
    for hd in range(MLA_HEADS):
        base = hd * HEAD_PAD
        k_ref[:, base:base + QK_NOPE] = kv[:, hd * QK_NOPE:(hd + 1) * QK_NOPE].astype(BF16)
        k_ref[:, base + QK_NOPE:base + HEAD_PAD] = krd_b
    vt_ref[...] = _transpose(kv[:, nw:nw + MLA_WIDTH]).astype(BF16)

    u = _gelu(z[:, o2:o3])
    v = _layer_norm(_gelu(z[:, o3:o4]), gv_ref[...], bv_ref[...])
    if vrow_refs:
        vrow_refs[0][...] = v
    vb = v.astype(BF16)
    cr = min(tm, GMLP_CHUNK)
    ri = lax.broadcasted_iota(jnp.int32, (cr, cr), 0) // CHUNK
    ci = lax.broadcasted_iota(jnp.int32, (cr, cr), 1) // CHUNK
    chunks = []
    for c in range(tm // cr):
        heads = []
        for hd in range(GMLP_HEADS):
            w = jnp.where(ri >= ci, ws_ref[hd, 0:cr, 0:cr], 0.0).astype(BF16)
            cols = slice(hd * GMLP_HEAD_DIM, (hd + 1) * GMLP_HEAD_DIM)
            mixed = _dot(w, vb[c * cr:(c + 1) * cr, cols]) + bs_ref[hd, 0:cr, :]
            heads.append(u[c * cr:(c + 1) * cr, cols] * mixed)
        chunks.append(jnp.concatenate(heads, axis=1))
    b_out = chunks[0] if len(chunks) == 1 else jnp.concatenate(chunks, axis=0)
    bn_ref[...] = _rms_norm(b_out, gob_ref[...]).astype(BF16)


def _input_projection(x, ln_g, ln_b, mod3, w, cos, sin, tm, with_vrows):
    b, t, d = x.shape
    assert t % tm == 0 and (tm % GMLP_CHUNK == 0 or (tm == t and t < GMLP_CHUNK))
    full = lambda a: pl.BlockSpec(a.shape, lambda bi, i: (0,) * a.ndim)
    tok = lambda width: pl.BlockSpec((None, tm, width), lambda bi, i: (bi, i, 0))
    weights = [w["w_in"], w["g_q"], w["g_kv"], w["w_uq"], w["w_ukv"], w["g_v"], w["b_v"], w["w_s"], w["b_s"],
               w["g_out_b"]]
    out_widths = [(MLA_HEADS * HEAD_PAD, BF16), (MLA_HEADS * HEAD_PAD, BF16), None,
                  (KV_LORA, F32), (QK_ROPE, F32), (GMLP_WIDTH, BF16)]
    if with_vrows:
        out_widths.append((GMLP_WIDTH, F32))
    out_specs = [pl.BlockSpec((None, MLA_WIDTH, tm), lambda bi, i: (bi, 0, i)) if o is None else tok(o[0])
                 for o in out_widths]
    out_shape = [jax.ShapeDtypeStruct((b, MLA_WIDTH, t), BF16) if o is None
                 else jax.ShapeDtypeStruct((b, t, o[0]), o[1]) for o in out_widths]
    return pl.pallas_call(
        functools.partial(_proj_kernel, d_model=d),
        grid=(b, t // tm),
        in_specs=[tok(d), full(ln_g), full(ln_b),
                  pl.BlockSpec((None, 1, mod3.shape[2]), lambda bi, i: (bi, 0, 0))]
                 + [full(a) for a in weights]
                 + [pl.BlockSpec((tm, cos.shape[1]), lambda bi, i: (i, 0))] * 2,
        out_specs=out_specs,
        out_shape=out_shape,
        compiler_params=_cparams(("parallel", "parallel")),
        name="input_projection",
    )(x, ln_g, ln_b, mod3, *weights, cos, sin)


def _cache_kernel(ckv_ref, kr_ref, wukv_ref, dup_ref, k_ref, vt_ref):
    kv = _dot(ckv_ref[...].astype(BF16), wukv_ref[...])
    krd = _dot(kr_ref[...].astype(BF16), dup_ref[...]).astype(BF16)
    nw = MLA_HEADS * QK_NOPE
    for hd in range(MLA_HEADS):
        base = hd * HEAD_PAD
        k_ref[:, base:base + QK_NOPE] = kv[:, hd * QK_NOPE:(hd + 1) * QK_NOPE].astype(BF16)
        k_ref[:, base + QK_NOPE:base + HEAD_PAD] = krd
    vt_ref[...] = _transpose(kv[:, nw:nw + MLA_WIDTH]).astype(BF16)


def _cache_projection(cache_ckv, cache_kr, w_ukv, dup, tm):
    b, p, _ = cache_ckv.shape
    assert p % tm == 0
    tok = lambda width: pl.BlockSpec((None, tm, width), lambda bi, i: (bi, i, 0))
    full = lambda a: pl.BlockSpec(a.shape, lambda bi, i: (0,) * a.ndim)
    return pl.pallas_call(
        _cache_kernel,
        grid=(b, p // tm),
        in_specs=[tok(KV_LORA), tok(QK_ROPE), full(w_ukv), full(dup)],
        out_specs=[tok(MLA_HEADS * HEAD_PAD), pl.BlockSpec((None, MLA_WIDTH, tm), lambda bi, i: (bi, 0, i))],
        out_shape=[jax.ShapeDtypeStruct((b, p, MLA_HEADS * HEAD_PAD), BF16),
                   jax.ShapeDtypeStruct((b, MLA_WIDTH, p), BF16)],
        compiler_params=_cparams(("parallel", "parallel")),
        name="cache_projection",
    )(cache_ckv, cache_kr, w_ukv, dup)


def _attn_kernel(q_ref, k_ref, vt_ref, ga_ref, o_ref, *, past, n_valid, tq, tk):
    i = pl.program_id(1)
    q_first = past + i * tq
    lo = jnp.minimum((q_first // CHUNK + 1) * CHUNK, n_valid)
    hi = jnp.minimum(((q_first + tq - 1) // CHUNK + 1) * CHUNK, n_valid)
    n_full = lo // tk
    n_tiles = (hi + tk - 1) // tk
    q_chunk = (q_first + lax.broadcasted_iota(jnp.int32, (tk, tq), 1)) // CHUNK
    k_iota = lax.broadcasted_iota(jnp.int32, (tk, tq), 0)

    def make_step(masked):
        def step(j, carry):
            off = pl.multiple_of(j * tk, tk)
            if masked:
                k_pos = off + k_iota
                visible = (k_pos // CHUNK <= q_chunk) & (k_pos < n_valid)
            heads = range(MLA_HEADS)
            scores = [_dot_nt(k_ref[pl.ds(off, tk), hd * HEAD_PAD:(hd + 1) * HEAD_PAD],
                              q_ref[:, hd * HEAD_PAD:(hd + 1) * HEAD_PAD]) for hd in heads]
            probs, stats = [], []
            for hd in heads:
                m, l, _ = carry[hd]
                s = scores[hd]
                if masked:
                    s = jnp.where(visible, s, NEG)
                m_new = jnp.maximum(m, jnp.max(s, axis=0, keepdims=True))
                alpha = jnp.exp2(m - m_new)
                p = jnp.exp2(s - m_new)
                stats.append((m_new, alpha * l + jnp.sum(p, axis=0, keepdims=True), alpha))
                probs.append(p.astype(BF16))
            new = []
            for hd in heads:
                m_new, l, alpha = stats[hd]
                vth = vt_ref[hd * V_HEAD:(hd + 1) * V_HEAD, pl.ds(off, tk)]
                new.append((m_new, l, alpha * carry[hd][2] + _dot(vth, probs[hd])))
            return tuple(new)
        return step

    init = tuple((jnp.full((1, tq), NEG, F32), jnp.zeros((1, tq), F32), jnp.zeros((V_HEAD, tq), F32))
                 for _ in range(MLA_HEADS))
    carry = lax.fori_loop(0, n_full, make_step(False), init)
    carry = lax.fori_loop(n_full, n_tiles, make_step(True), carry)
    a_t = jnp.concatenate([acc / l for _, l, acc in carry], axis=0)
    o_ref[...] = _rms_norm(_transpose(a_t), ga_ref[...]).astype(BF16)


def _attention(q, k, vt, g_out_a, past, n_valid, tq, tk):
    b, t, _ = q.shape
    lk = k.shape[1]
    assert t % tq == 0 and lk % tk == 0 and n_valid <= lk and vt.shape[2] == lk
    return pl.pallas_call(
        functools.partial(_attn_kernel, past=past, n_valid=n_valid, tq=tq, tk=tk),
        grid=(b, t // tq),
        in_specs=[pl.BlockSpec((None, tq, q.shape[2]), lambda bi, i: (bi, i, 0)),
                  pl.BlockSpec((None, lk, k.shape[2]), lambda bi, i: (bi, 0, 0)),
                  pl.BlockSpec((None, vt.shape[1], lk), lambda bi, i: (bi, 0, 0)),
                  pl.BlockSpec(g_out_a.shape, lambda bi, i: (0, 0))],
        out_specs=pl.BlockSpec((None, tq, MLA_WIDTH), lambda bi, i: (bi, i, 0)),
        out_shape=jax.ShapeDtypeStruct((b, t, MLA_WIDTH), BF16),
        compiler_params=_cparams(("parallel", "arbitrary")),
        name="attention",
    )(q, k, vt, g_out_a)


def _extract_top(s, rounds):
    r_, l_ = s.shape
    ridx = lax.broadcasted_iota(jnp.int32, (r_, l_), 0).astype(F32)
    vidx = lax.broadcasted_iota(jnp.int32, (rounds, l_), 0)
    rank = jnp.full((r_, l_), float(rounds), F32)
    vals = jnp.zeros((rounds, l_), F32)
    for r in range(rounds):
        m = jnp.max(s, axis=0, keepdims=True)
        first = jnp.min(jnp.where(s == m, ridx, float(r_)), axis=0, keepdims=True)
        hit = ridx == first
        rank = jnp.where(hit, float(r), rank)
        s = jnp.where(hit, LOWEST, s)
        vals = jnp.where(vidx == r, m, vals)
    return rank, vals


def _route_kernel(x_ref, an_ref, bn_ref, lng_ref, lnb_ref, mod_ref, wo_ref, l1g_ref, l1b_ref, wpq_ref, sk_ref,
                  grp_ref, x1_ref, h2t_ref, rank2_ref, e2_ref, nrow_ref, e1z_ref, cand_ref,
                  *, d_model, alpha, seg):
    d = d_model
    tm = x_ref.shape[0]
    h0 = _layer_norm(x_ref[...], lng_ref[...], lnb_ref[...])
    mix = _dot(an_ref[...], wo_ref[0:MLA_WIDTH, :]) + _dot(bn_ref[...], wo_ref[MLA_WIDTH:, :])
    parts = []
    for sgm in range(tm // seg):
        rows = slice(sgm * seg, (sgm + 1) * seg)
        gt1 = mod_ref[sgm, :, 2 * d:3 * d]
        sh2 = mod_ref[sgm, :, 3 * d:4 * d]
        sc2 = mod_ref[sgm, :, 4 * d:5 * d]
        x1 = _layer_norm(alpha * h0[rows] + (1.0 + gt1) * mix[rows], l1g_ref[...], l1b_ref[...])
        x1_ref[rows, :] = x1
        parts.append(x1 * (1.0 + sc2) + sh2)
    h2 = parts[0] if len(parts) == 1 else jnp.concatenate(parts, axis=0)
    h2t = h2.T.astype(BF16)
    h2t_ref[...] = h2t
    qpt = _dot(wpq_ref[...], h2t).astype(BF16)
    half = D_KEY // 2

    pad0 = sum(CAND_COUNTS) // 8 * 8
    cand_ref[pad0:CAND_ROWS, :] = jnp.full((CAND_ROWS - pad0, tm), LOWEST, F32)
    for hd in range(PEER_HEADS):
        base = hd * D_KEY
        s1 = _dot(sk_ref[2 * hd], qpt[base:base + half, :])
        s2 = _dot(sk_ref[2 * hd + 1], qpt[base + half:base + D_KEY, :])
        rank1, a = _extract_top(s1, PEER_TOPK)
        rank2, bvals = _extract_top(s2, PEER_TOPK)
        off = 0
        for k1, cnt in enumerate(CAND_COUNTS):
            cand_ref[off:off + cnt, :] = a[k1:k1 + 1, :] + bvals[0:cnt, :]
            off += cnt
        cand = cand_ref[...]
        crank, _ = _extract_top(cand, PEER_TOPK)
        sel = crank < float(PEER_TOPK)
        z = jnp.sum(jnp.where(sel, jnp.exp(cand - cand[0:1, :]), 0.0), axis=0, keepdims=True)
        n = _dot(grp_ref[...], jnp.where(sel, 1.0, 0.0).astype(BF16))
        nrow = jnp.zeros((N_KEYS, tm), F32)
        for k1 in range(PEER_TOPK):
            nrow = jnp.where(rank1 == float(k1), n[k1:k1 + 1, :], nrow)
        rank2_ref[hd] = rank2.astype(BF16)
        e2_ref[hd] = jnp.exp(s2 - bvals[0:1, :]).astype(BF16)
        nrow_ref[hd] = nrow
        e1z_ref[hd] = jnp.exp(s1 - a[0:1, :]) / z


def _route(x2d, an2d, bn2d, ln_g, ln_b, mod3, w, tm, seg, t_batch, alpha):
    n, d = x2d.shape
    assert n % tm == 0 and tm % seg == 0 and tm % LANES == 0
    nseg = tm // seg
    tok = lambda width: pl.BlockSpec((tm, width), lambda i: (i, 0))
    full = lambda a: pl.BlockSpec(a.shape, lambda i: (0,) * a.ndim)
    tbl = pl.BlockSpec((PEER_HEADS, N_KEYS, tm), lambda i: (0, 0, i))
    weights = [w["w_o"], w["ln1_g"], w["ln1_b"], w["w_pq_t"], w["sub_keys"], w["cand_groups"]]
    return pl.pallas_call(
        functools.partial(_route_kernel, d_model=d, alpha=alpha, seg=seg),
        grid=(n // tm,),
        in_specs=[tok(d), tok(MLA_WIDTH), tok(GMLP_WIDTH), full(ln_g), full(ln_b),
                  pl.BlockSpec((nseg, 1, mod3.shape[2]), lambda i: ((i * tm // t_batch) // nseg, 0, 0))]
                 + [full(a) for a in weights],
        out_specs=[tok(d), pl.BlockSpec((d, tm), lambda i: (0, i)), tbl, tbl, tbl, tbl],
        out_shape=[jax.ShapeDtypeStruct((n, d), F32), jax.ShapeDtypeStruct((d, n), BF16),
                   jax.ShapeDtypeStruct((PEER_HEADS, N_KEYS, n), BF16),
                   jax.ShapeDtypeStruct((PEER_HEADS, N_KEYS, n), BF16),
                   jax.ShapeDtypeStruct((PEER_HEADS, N_KEYS, n), F32),
                   jax.ShapeDtypeStruct((PEER_HEADS, N_KEYS, n), F32)],
        scratch_shapes=[pltpu.VMEM((CAND_ROWS, tm), F32)],
        compiler_params=_cparams(("parallel",)),
        name="route",
    )(x2d, an2d, bn2d, ln_g, ln_b, mod3, *weights)


def _peer_kernel(h2t_ref, rank2_ref, e2_ref, nrow_ref, e1z_ref, u_ref, vt_ref, o_ref, a_ref, *, tw):
    c = pl.program_id(1)
    tb = h2t_ref.shape[1]
    rows_per_chunk = u_ref.shape[0] // N_KEYS
    cur = c % 2
    prev = 1 - cur

    @pl.when(c == 0)
    def _():
        o_ref[...] = jnp.zeros_like(o_ref)
        a_ref[1] = jnp.zeros(a_ref.shape[1:], BF16)

    def subtile(ts, carry):
        off = pl.multiple_of(ts * tw, tw)
        o_ref[:, pl.ds(off, tw)] += _dot(vt_ref[...], a_ref[prev, :, pl.ds(off, tw)])
        ht = h2t_ref[:, pl.ds(off, tw)]
        for p in range(rows_per_chunk):
            s = _dot(u_ref[p * N_KEYS:(p + 1) * N_KEYS, :], ht)
            wgt = jnp.zeros((N_KEYS // BF16_ROWS, BF16_ROWS, tw), BF16)
            for hd in range(PEER_HEADS):
                nr = jnp.broadcast_to(nrow_ref[hd, p:p + 1, pl.ds(off, tw)], (BF16_ROWS, tw)).astype(BF16)
                e1 = jnp.broadcast_to(e1z_ref[hd, p:p + 1, pl.ds(off, tw)], (BF16_ROWS, tw)).astype(BF16)
                r2 = rank2_ref[hd, :, :, pl.ds(off, tw)]
                e2 = e2_ref[hd, :, :, pl.ds(off, tw)]
                wgt = wgt + jnp.where(r2 < nr[None], e2, jnp.zeros_like(e2)) * e1[None]
            a_ref[cur, p * N_KEYS:(p + 1) * N_KEYS, pl.ds(off, tw)] = (
                _gelu(s).astype(BF16) * wgt.reshape(N_KEYS, tw))
        return carry

    lax.fori_loop(0, tb // tw, subtile, 0)


def _peer_mix(h2t, rank2, e2, nrow, e1z, u_b, vt_b, tb, tw, ec):
    d, n = h2t.shape
    ne = u_b.shape[0]
    assert n % tb == 0 and tb % tw == 0 and ne % ec == 0 and ec % (8 * N_KEYS) == 0
    rows = ec // N_KEYS
    groups = N_KEYS // BF16_ROWS
    tbl = pl.BlockSpec((PEER_HEADS, groups, BF16_ROWS, tb), lambda b, c: (0, 0, 0, b))
    rank2 = rank2.reshape(PEER_HEADS, groups, BF16_ROWS, n)
    e2 = e2.reshape(PEER_HEADS, groups, BF16_ROWS, n)
    last = ne // ec - 1
    row = pl.BlockSpec((PEER_HEADS, rows, tb), lambda b, c: (0, jnp.minimum(c, last), b))
    return pl.pallas_call(
        functools.partial(_peer_kernel, tw=tw),
        grid=(n // tb, ne // ec + 1),
        in_specs=[pl.BlockSpec((d, tb), lambda b, c: (0, b)), tbl, tbl, row, row,
                  pl.BlockSpec((ec, d), lambda b, c: (jnp.minimum(c, last), 0)),
                  pl.BlockSpec((d, ec), lambda b, c: (0, jnp.maximum(c - 1, 0)))],
        out_specs=pl.BlockSpec((d, tb), lambda b, c: (0, b)),
        out_shape=jax.ShapeDtypeStruct((d, n), F32),
        scratch_shapes=[pltpu.VMEM((2, ec, tb), BF16)],
        compiler_params=_cparams(("parallel", "arbitrary")),
        name="peer_mix",
    )(h2t, rank2, e2, nrow, e1z, u_b, vt_b)


def _final_kernel(ft_ref, x1_ref, mod_ref, g_ref, b_ref, o_ref, *, d_model, alpha, seg):
    d = d_model
    tm = x1_ref.shape[0]
    f = ft_ref[...].T
    for sgm in range(tm // seg):
        rows = slice(sgm * seg, (sgm + 1) * seg)
        gt2 = mod_ref[sgm, :, 5 * d:6 * d]
        o_ref[rows, :] = _layer_norm(alpha * x1_ref[rows, :] + (1.0 + gt2) * f[rows], g_ref[...], b_ref[...])


def _final(ft, x1, mod3, g, b, tm, seg, t_batch, alpha):
    n, d = x1.shape
    nseg = tm // seg
    return pl.pallas_call(
        functools.partial(_final_kernel, d_model=d, alpha=alpha, seg=seg),
        grid=(n // tm,),
        in_specs=[pl.BlockSpec((d, tm), lambda i: (0, i)), pl.BlockSpec((tm, d), lambda i: (i, 0)),
                  pl.BlockSpec((nseg, 1, mod3.shape[2]), lambda i: ((i * tm // t_batch) // nseg, 0, 0)),
                  pl.BlockSpec(g.shape, lambda i: (0, 0)), pl.BlockSpec(b.shape, lambda i: (0, 0))],
        out_specs=pl.BlockSpec((tm, d), lambda i: (i, 0)),
        out_shape=jax.ShapeDtypeStruct((n, d), F32),
        compiler_params=_cparams(("parallel",)),
        name="final_norm",
    )(ft, x1, mod3, g, b)


def _rope_tables(past, t):
    half = QK_ROPE // 2
    inv_freq = jnp.power(jnp.float32(ROPE_BASE), -jnp.arange(half, dtype=F32) * (2.0 / QK_ROPE))
    pos = past + jnp.arange(t, dtype=jnp.int32)
    ang = pos.astype(F32)[:, None] * inv_freq[None, :]
    cos = jnp.cos(ang)
    sin = jnp.sin(ang)
    cos_t = jnp.tile(jnp.concatenate([cos, cos], axis=1), (1, MLA_HEADS))
    sin_t = jnp.tile(jnp.concatenate([-sin, sin], axis=1), (1, MLA_HEADS))
    return cos_t, sin_t


def _swap_halves(w):
    half = QK_ROPE // 2
    return jnp.concatenate([w[..., half:], w[..., :half]], axis=-1)


def _prepare_layer(w_in, g_q, g_kv, w_uq, w_uk, w_uv, g_v, b_v, w_s, b_s, g_out_a, g_out_b, w_o,
                   ln1_g, ln1_b, w_pq, sub_keys, peer_u, peer_v, ln2_g, ln2_b):
    o1 = Q_LORA
    o2 = o1 + KV_LORA
    o3 = o2 + QK_ROPE
    o4 = o3 + GMLP_WIDTH
    kr_w = w_in[:, o2:o3]
    kr_sw = _swap_halves(kr_w)
    w_in_x = jnp.concatenate([w_in[:, :o2], w_in[:, o3:o4], w_in[:, o4:], kr_w, kr_w, kr_sw, kr_sw], axis=1)
    uq = w_uq.reshape(Q_LORA, MLA_HEADS, QK_NOPE + QK_ROPE)
    uq_rope = uq[:, :, QK_NOPE:]
    w_uq_x = jnp.concatenate([uq[:, :, :QK_NOPE].reshape(Q_LORA, -1), uq_rope.reshape(Q_LORA, -1),
                              _swap_halves(uq_rope).reshape(Q_LORA, -1)], axis=1)
    row = lambda a: a.reshape(1, -1)
    groups = []
    for k1, cnt in enumerate(CAND_COUNTS):
        groups += [k1] * cnt
    groups += [-1] * (CAND_ROWS - len(groups))
    cand_groups = (jnp.arange(PEER_TOPK)[:, None] == jnp.asarray(groups)[None, :]).astype(BF16)
    return dict(
        w_in=w_in_x.astype(BF16), g_q=row(g_q), g_kv=row(g_kv), w_uq=w_uq_x.astype(BF16),
        w_ukv=jnp.concatenate([w_uk, w_uv], axis=1).astype(BF16), g_v=row(g_v), b_v=row(b_v), w_s=w_s,
        b_s=jnp.broadcast_to(b_s[:, :, None], b_s.shape + (GMLP_HEAD_DIM,)), g_out_a=row(g_out_a),
        g_out_b=row(g_out_b), w_o=w_o.astype(BF16), ln1_g=row(ln1_g), ln1_b=row(ln1_b),
        w_pq_t=w_pq.T.astype(BF16), sub_keys=sub_keys.reshape(PEER_HEADS * 2, N_KEYS, D_KEY // 2).astype(BF16),
        cand_groups=cand_groups, peer_u=peer_u.astype(BF16), peer_vt=peer_v.T.astype(BF16),
        ln2_g=row(ln2_g), ln2_b=row(ln2_b),
        kr_dup=jnp.concatenate([jnp.eye(QK_ROPE, dtype=BF16)] * 2, axis=1),
    )


def _pick_tile(n, prefs):
    for p in prefs:
        if n % p == 0:
            return p
    raise ValueError(f"no tile for {n}")


def _layer_group(x, mod, cache_ckv, cache_kr, w, ln_g, ln_b, alpha):
    b, t, d = x.shape
    n = b * t
    past = 0 if cache_ckv is None else cache_ckv.shape[1]
    mod3 = mod.reshape(b, 1, mod.shape[1])
    cos, sin = _rope_tables(past, t)

    tm = _pick_tile(t, (256, 128)) if t >= GMLP_CHUNK else t
    outs = _input_projection(x, ln_g, ln_b, mod3, w, cos, sin, tm, cache_ckv is not None)
    q, k_new, v_new, ckv, kr, bn = outs[:6]
    v_rows = outs[6] if cache_ckv is not None else None

    tq = _pick_tile(t, (512, 256, 128, 64))
    if cache_ckv is None:
        tk = _pick_tile(t, (512, 256))
        k_all, v_all, n_valid = k_new, v_new, t
    else:
        k_c, v_c = _cache_projection(cache_ckv, cache_kr, w["w_ukv"], w["kr_dup"], _pick_tile(past, (512, 256, 128)))
        n_valid = past + t
        pad = (-n_valid) % 256
        tk = _pick_tile(n_valid + pad, (768, 512, 256))
        k_all = jnp.concatenate([k_c, k_new, jnp.zeros((b, pad, k_new.shape[2]), BF16)], axis=1)
        v_all = jnp.concatenate([v_c, v_new, jnp.zeros((b, v_new.shape[1], pad), BF16)], axis=2)
    an = _attention(q, k_all, v_all, w["g_out_a"], past, n_valid, tq, tk)

    tr = _pick_tile(n, (256, 128))
    seg = min(t, tr)
    assert tr % seg == 0 and t % seg == 0
    x1, h2t, rank2, e2, nrow, e1z = _route(x.reshape(n, d), an.reshape(n, -1), bn.reshape(n, -1), ln_g, ln_b,
                                          mod3, w, tr, seg, t, alpha)
    tb = _pick_tile(n, (512, 256))
    ft = _peer_mix(h2t, rank2, e2, nrow, e1z, w["peer_u"], w["peer_vt"], tb, 256, 8 * N_KEYS)
    y = _final(ft, x1, mod3, w["ln2_g"], w["ln2_b"], tr, seg, t, alpha)
    return y.reshape(b, t, d), ckv, kr, v_rows


def kernel(x_prompt, x_sample, cache_ckv, cache_krope, c_prompt, c_sample, ln_in_g, ln_in_b, w_ada, b_ada, w_in,
           g_q, g_kv, w_uq, w_uk, w_uv, g_v, b_v, w_s, b_s, g_out_a, g_out_b, w_o, ln1_g, ln1_b, w_pq, sub_keys,
           peer_u, peer_v, ln2_g, ln2_b):
    depth = w_ada.shape[0]
    assert depth == 1, "the entry LayerNorm is fused into the first layer's kernels"
    alpha = (2 * depth) ** 0.25
    bp = c_prompt.shape[0]
    bs = c_sample.shape[0]
    c_all = jnp.concatenate([c_prompt, c_sample], axis=0)
    c_all = jnp.pad(c_all, ((0, (-c_all.shape[0]) % 8), (0, 0)))
    ln_g = ln_in_g.reshape(1, -1)
    ln_b = ln_in_b.reshape(1, -1)
    hp, hs = x_prompt, x_sample
    ckv_p, kr_p, ckv_s, kr_s, gv_s = [], [], [], [], []
    for l in range(depth):
        w = _prepare_layer(w_in[l], g_q[l], g_kv[l], w_uq[l], w_uk[l], w_uv[l], g_v[l], b_v[l], w_s[l], b_s[l],
                           g_out_a[l], g_out_b[l], w_o[l], ln1_g[l], ln1_b[l], w_pq[l], sub_keys[l], peer_u[l],
                           peer_v[l], ln2_g[l], ln2_b[l])
        mod = _ada_mod(c_all, w_ada[l], b_ada[l])
        hp, ckv_new_p, kr_new_p, _ = _layer_group(hp, mod[:bp], None, None, w, ln_g, ln_b, alpha)
        hs, ckv_new_s, kr_new_s, v_new_s = _layer_group(hs, mod[bp:bp + bs], cache_ckv[l], cache_krope[l], w,
                                                       ln_g, ln_b, alpha)
        ckv_p.append(ckv_new_p)
        kr_p.append(kr_new_p)
        ckv_s.append(ckv_new_s)
        kr_s.append(kr_new_s)
        gv_s.append(v_new_s)
    return (hp, hs, jnp.stack(ckv_p), jnp.stack(kr_p), jnp.stack(ckv_s), jnp.stack(kr_s), jnp.stack(gv_s))
```

```python
import functools
import math

import jax
import jax.numpy as jnp
from jax import lax
from jax.experimental import pallas as pl
from jax.experimental.pallas import tpu as pltpu

F32 = jnp.float32
BF16 = jnp.bfloat16

CHUNK = 64
MLA_HEADS = 4
QK_NOPE = 128
QK_ROPE = 64
V_HEAD = 128
Q_LORA = 768
KV_LORA = 256
ROPE_BASE = 10000.0
HEAD_PAD = 2 * QK_NOPE
GMLP_HEADS = 4
GMLP_HEAD_DIM = 128
GMLP_CHUNK = 128
MLA_WIDTH = MLA_HEADS * V_HEAD
GMLP_WIDTH = GMLP_HEADS * GMLP_HEAD_DIM
PEER_HEADS = 8
N_KEYS = 128
PEER_TOPK = 16
D_KEY = 256
EPS = 1e-6
NEG = -1e30
LOWEST = -3.0e38

LANES = 128
SUBLANES = 8
BF16_ROWS = 16
VMEM_LIMIT = 56 * 1024 * 1024

CAND_COUNTS = tuple(PEER_TOPK // (k + 1) for k in range(PEER_TOPK))
CAND_ROWS = 64


def _cparams(sem):
    return pltpu.CompilerParams(dimension_semantics=sem, vmem_limit_bytes=VMEM_LIMIT)


def _gelu(x):
    c1 = -2.0 * math.sqrt(2.0 / math.pi) * math.log2(math.e)
    return x / (1.0 + jnp.exp2(x * (c1 + (c1 * 0.044715) * (x * x))))


def _layer_norm(x, g, b):
    mu = jnp.mean(x, axis=-1, keepdims=True)
    xc = x - mu
    var = jnp.mean(xc * xc, axis=-1, keepdims=True)
    return xc * lax.rsqrt(var + EPS) * g + b


def _rms_norm(x, g):
    ms = jnp.mean(x * x, axis=-1, keepdims=True)
    return x * lax.rsqrt(ms + EPS) * g


def _dot(a, b):
    return jnp.dot(a, b, preferred_element_type=F32)


def _transpose(x):
    return x.T


def _dot_nt(a, b):
    return lax.dot_general(a, b, (((1,), (1,)), ((), ())), preferred_element_type=F32)


def _ada_kernel(c_ref, w_ref, b_ref, o_ref):
    c = c_ref[...]
    s = c * (1.0 / (1.0 + jnp.exp(-c)))
    o_ref[...] = _dot(s.astype(BF16), w_ref[...].astype(BF16)) + b_ref[...]


def _ada_mod(c, w_ada, b_ada):
    bc, d = c.shape
    n = w_ada.shape[1]
    tn = 1536
    return pl.pallas_call(
        _ada_kernel,
        grid=(n // tn,),
        in_specs=[pl.BlockSpec((bc, d), lambda j: (0, 0)),
                  pl.BlockSpec((d, tn), lambda j: (0, j)),
                  pl.BlockSpec((1, tn), lambda j: (0, j))],
        out_specs=pl.BlockSpec((bc, tn), lambda j: (0, j)),
        out_shape=jax.ShapeDtypeStruct((bc, n), F32),
        compiler_params=_cparams(("arbitrary",)),
        name="ada_mod",
    )(c, w_ada, b_ada.reshape(1, n))


def _proj_kernel(x_ref, lng_ref, lnb_ref, mod_ref, win_ref, gq_ref, gkv_ref, wuq_ref, wukv_ref,
                 gv_ref, bv_ref, ws_ref, bs_ref, gob_ref, cos_ref, sin_ref,
                 q_ref, k_ref, vt_ref, ckv_ref, kr_ref, bn_ref, *vrow_refs, d_model):
    d = d_model
    tm = x_ref.shape[0]
    h0 = _layer_norm(x_ref[...], lng_ref[...], lnb_ref[...])
    sh1 = mod_ref[:, 0:d]
    sc1 = mod_ref[:, d:2 * d]
    h = h0 * (1.0 + sc1) + sh1
    z = _dot(h.astype(BF16), win_ref[...])
    o1 = Q_LORA
    o2 = o1 + KV_LORA
    o3 = o2 + GMLP_WIDTH
    o4 = o3 + GMLP_WIDTH
    o5 = o4 + LANES
    cos = cos_ref[...]
    sin = sin_ref[...]

    cq = _rms_norm(z[:, 0:o1], gq_ref[...])
    qa = _dot(cq.astype(BF16), wuq_ref[...])
    nw = MLA_HEADS * QK_NOPE
    rw = MLA_HEADS * QK_ROPE
    scale = math.log2(math.e) / math.sqrt(QK_NOPE + QK_ROPE)
    qr = (qa[:, nw:nw + rw] * cos + qa[:, nw + rw:nw + 2 * rw] * sin) * scale
    lane = lax.broadcasted_iota(jnp.int32, (tm, LANES), 1)
    for hd in range(MLA_HEADS):
        base = hd * HEAD_PAD
        q_ref[:, base:base + QK_NOPE] = (qa[:, hd * QK_NOPE:(hd + 1) * QK_NOPE] * scale).astype(BF16)
        tile = qr[:, (hd // 2) * LANES:(hd // 2 + 1) * LANES]
        keep = (lane < QK_ROPE) if hd % 2 == 0 else (lane >= QK_ROPE)
        q_ref[:, base + QK_NOPE:base + HEAD_PAD] = jnp.where(keep, tile, 0.0).astype(BF16)

    ckv = _rms_norm(z[:, o1:o2], gkv_ref[...])
    ckv_ref[...] = ckv
    kv = _dot(ckv.astype(BF16), wukv_ref[...])
    krd = z[:, o4:o5] * cos[:, 0:LANES] + z[:, o5:o5 + LANES] * sin[:, 0:LANES]
    kr_ref[...] = krd[:, 0:QK_ROPE]
    krd_b = krd.astype(BF16)
    for hd in range(MLA_HEADS):
        base = hd * HEAD_PAD
        k_ref[:, base:base + QK_NOPE] = kv[:, hd * QK_NOPE:(hd + 1) * QK_NOPE].astype(BF16)
        k_ref[:, base + QK_NOPE:base + HEAD_PAD] = krd_b
    vt_ref[...] = _transpose(kv[:, nw:nw + MLA_WIDTH]).astype(BF16)

    u = _gelu(z[:, o2:o3])
    v = _layer_norm(_gelu(z[:, o3:o4]), gv_ref[...], bv_ref[...])
    if vrow_refs:
        vrow_refs[0][...] = v
    vb = v.astype(BF16)
    cr = min(tm, GMLP_CHUNK)
    ri = lax.broadcasted_iota(jnp.int32, (cr, cr), 0) // CHUNK
    ci = lax.broadcasted_iota(jnp.int32, (cr, cr), 1) // CHUNK
    chunks = []
    for c in range(tm // cr):
        heads = []
        for hd in range(GMLP_HEADS):
            w = jnp.where(ri >= ci, ws_ref[hd, 0:cr, 0:cr], 0.0).astype(BF16)
            cols = slice(hd * GMLP_HEAD_DIM, (hd + 1) * GMLP_HEAD_DIM)
            mixed = _dot(w, vb[c * cr:(c + 1) * cr, cols]) + bs_ref[hd, 0:cr, :]
            heads.append(u[c * cr:(c + 1) * cr, cols] * mixed)
        chunks.append(jnp.concatenate(heads, axis=1))
    b_out = chunks[0] if len(chunks) == 1 else jnp.concatenate(chunks, axis=0)
    bn_ref[...] = _rms_norm(b_out, gob_ref[...]).astype(BF16)


def _input_projection(x, ln_g, ln_b, mod3, w, cos, sin, tm, with_vrows):
    b, t, d = x.shape
    assert t % tm == 0 and (tm % GMLP_CHUNK == 0 or (tm == t and t < GMLP_CHUNK))
    full = lambda a: pl.BlockSpec(a.shape, lambda bi, i: (0,) * a.ndim)
    tok = lambda width: pl.BlockSpec((None, tm, width), lambda bi, i: (bi, i, 0))
    weights = [w["w_in"], w["g_q"], w["g_kv"], w["w_uq"], w["w_ukv"], w["g_v"], w["b_v"], w["w_s"], w["b_s"],
               w["g_out_b"]]
    out_widths = [(MLA_HEADS * HEAD_PAD, BF16), (MLA_HEADS * HEAD_PAD, BF16), None,
                  (KV_LORA, F32), (QK_ROPE, F32), (GMLP_WIDTH, BF16)]
    if with_vrows:
        out_widths.append((GMLP_WIDTH, F32))
    out_specs = [pl.BlockSpec((None, MLA_WIDTH, tm), lambda bi, i: (bi, 0, i)) if o is None else tok(o[0])
                 for o in out_widths]
    out_shape = [jax.ShapeDtypeStruct((b, MLA_WIDTH, t), BF16) if o is None
                 else jax.ShapeDtypeStruct((b, t, o[0]), o[1]) for o in out_widths]
    return pl.pallas_call(
        functools.partial(_proj_kernel, d_model=d),
        grid=(b, t // tm),
        in_specs=[tok(d), full(ln_g), full(ln_b),
                  pl.BlockSpec((None, 1, mod3.shape[2]), lambda bi, i: (bi, 0, 0))]
                 + [full(a) for a in weights]
                 + [pl.BlockSpec((tm, cos.shape[1]), lambda bi, i: (i, 0))] * 2,
        out_specs=out_specs,
        out_shape=out_shape,
        compiler_params=_cparams(("parallel", "parallel")),
        name="input_projection",
    )(x, ln_g, ln_b, mod3, *weights, cos, sin)


def _cache_kernel(ckv_ref, kr_ref, wukv_ref, dup_ref, k_ref, vt_ref):
    kv = _dot(ckv_ref[...].astype(BF16), wukv_ref[...])
    krd = _dot(kr_ref[...].astype(BF16), dup_ref[...]).astype(BF16)
    nw = MLA_HEADS * QK_NOPE
    for hd in range(MLA_HEADS):
        base = hd * HEAD_PAD
        k_ref[:, base:base + QK_NOPE] = kv[:, hd * QK_NOPE:(hd + 1) * QK_NOPE].astype(BF16)
        k_ref[:, base + QK_NOPE:base + HEAD_PAD] = krd
    vt_ref[...] = _transpose(kv[:, nw:nw + MLA_WIDTH]).astype(BF16)


def _cache_projection(cache_ckv, cache_kr, w_ukv, dup, tm):
    b, p, _ = cache_ckv.shape
    assert p % tm == 0
    tok = lambda width: pl.BlockSpec((None, tm, width), lambda bi, i: (bi, i, 0))
    full = lambda a: pl.BlockSpec(a.shape, lambda bi, i: (0,) * a.ndim)
    return pl.pallas_call(
        _cache_kernel,
        grid=(b, p // tm),
        in_specs=[tok(KV_LORA), tok(QK_ROPE), full(w_ukv), full(dup)],
        out_specs=[tok(MLA_HEADS * HEAD_PAD), pl.BlockSpec((None, MLA_WIDTH, tm), lambda bi, i: (bi, 0, i))],
        out_shape=[jax.ShapeDtypeStruct((b, p, MLA_HEADS * HEAD_PAD), BF16),
                   jax.ShapeDtypeStruct((b, MLA_WIDTH, p), BF16)],
        compiler_params=_cparams(("parallel", "parallel")),
        name="cache_projection",
    )(cache_ckv, cache_kr, w_ukv, dup)


def _attn_kernel(q_ref, k_ref, vt_ref, ga_ref, o_ref, *, past, n_valid, tq, tk):
    i = pl.program_id(1)
    q_first = past + i * tq
    lo = jnp.minimum((q_first // CHUNK + 1) * CHUNK, n_valid)
    hi = jnp.minimum(((q_first + tq - 1) // CHUNK + 1) * CHUNK, n_valid)
    n_full = lo // tk
    n_tiles = (hi + tk - 1) // tk
    q_chunk = (q_first + lax.broadcasted_iota(jnp.int32, (tk, tq), 1)) // CHUNK
    k_iota = lax.broadcasted_iota(jnp.int32, (tk, tq), 0)

    def make_step(masked):
        def step(j, carry):
            off = pl.multiple_of(j * tk, tk)
            if masked:
                k_pos = off + k_iota
                visible = (k_pos // CHUNK <= q_chunk) & (k_pos < n_valid)
            heads = range(MLA_HEADS)
            scores = [_dot_nt(k_ref[pl.ds(off, tk), hd * HEAD_PAD:(hd + 1) * HEAD_PAD],
                              q_ref[:, hd * HEAD_PAD:(hd + 1) * HEAD_PAD]) for hd in heads]
            probs, stats = [], []
            for hd in heads:
                m, l, _ = carry[hd]
                s = scores[hd]
                if masked:
                    s = jnp.where(visible, s, NEG)
                m_new = jnp.maximum(m, jnp.max(s, axis=0, keepdims=True))
                alpha = jnp.exp2(m - m_new)
                p = jnp.exp2(s - m_new)
                stats.append((m_new, alpha * l + jnp.sum(p, axis=0, keepdims=True), alpha))
                probs.append(p.astype(BF16))
            new = []
            for hd in heads:
                m_new, l, alpha = stats[hd]
                vth = vt_ref[hd * V_HEAD:(hd + 1) * V_HEAD, pl.ds(off, tk)]
                new.append((m_new, l, alpha * carry[hd][2] + _dot(vth, probs[hd])))
            return tuple(new)
        return step

    init = tuple((jnp.full((1, tq), NEG, F32), jnp.zeros((1, tq), F32), jnp.zeros((V_HEAD, tq), F32))
                 for _ in range(MLA_HEADS))
    carry = lax.fori_loop(0, n_full, make_step(False), init)
    carry = lax.fori_loop(n_full, n_tiles, make_step(True), carry)
    a_t = jnp.concatenate([acc / l for _, l, acc in carry], axis=0)
    o_ref[...] = _rms_norm(_transpose(a_t), ga_ref[...]).astype(BF16)


def _attention(q, k, vt, g_out_a, past, n_valid, tq, tk):
    b, t, _ = q.shape
    lk = k.shape[1]
    assert t % tq == 0 and lk % tk == 0 and n_valid <= lk and vt.shape[2] == lk
    return pl.pallas_call(
        functools.partial(_attn_kernel, past=past, n_valid=n_valid, tq=tq, tk=tk),
        grid=(b, t // tq),
        in_specs=[pl.BlockSpec((None, tq, q.shape[2]), lambda bi, i: (bi, i, 0)),
                  pl.BlockSpec((None, lk, k.shape[2]), lambda bi, i: (bi, 0, 0)),
                  pl.BlockSpec((None, vt.shape[1], lk), lambda bi, i: (bi, 0, 0)),
                  pl.BlockSpec(g_out_a.shape, lambda bi, i: (0, 0))],
        out_specs=pl.BlockSpec((None, tq, MLA_WIDTH), lambda bi, i: (bi, i, 0)),
        out_shape=jax.ShapeDtypeStruct((b, t, MLA_WIDTH), BF16),
        compiler_params=_cparams(("parallel", "arbitrary")),
        name="attention",
    )(q, k, vt, g_out_a)


def _extract_top(s, rounds):
    r_, l_ = s.shape
    ridx = lax.broadcasted_iota(jnp.int32, (r_, l_), 0).astype(F32)
    vidx = lax.broadcasted_iota(jnp.int32, (rounds, l_), 0)
    rank = jnp.full((r_, l_), float(rounds), F32)
    vals = jnp.zeros((rounds, l_), F32)
    for r in range(rounds):
        m = jnp.max(s, axis=0, keepdims=True)
        first = jnp.min(jnp.where(s == m, ridx, float(r_)), axis=0, keepdims=True)
        hit = ridx == first
        rank = jnp.where(hit, float(r), rank)
        s = jnp.where(hit, LOWEST, s)
        vals = jnp.where(vidx == r, m, vals)
    return rank, vals


def _route_kernel(x_ref, an_ref, bn_ref, lng_ref, lnb_ref, mod_ref, wo_ref, l1g_ref, l1b_ref, wpq_ref, sk_ref,
                  skbd_ref, perm_ref, grp_ref, x1_ref, h2t_ref, rank2_ref, e2_ref, nrow_ref, e1z_ref, qpt_ref,
                  cand_ref, *, d_model, alpha, seg):
    d = d_model
    tm = x_ref.shape[0]
    h0 = _layer_norm(x_ref[...], lng_ref[...], lnb_ref[...])
    mix = _dot(an_ref[...], wo_ref[0:MLA_WIDTH, :]) + _dot(bn_ref[...], wo_ref[MLA_WIDTH:, :])
    parts = []
    for sgm in range(tm // seg):
        rows = slice(sgm * seg, (sgm + 1) * seg)
        gt1 = mod_ref[sgm, :, 2 * d:3 * d]
        sh2 = mod_ref[sgm, :, 3 * d:4 * d]
        sc2 = mod_ref[sgm, :, 4 * d:5 * d]
        x1 = _layer_norm(alpha * h0[rows] + (1.0 + gt1) * mix[rows], l1g_ref[...], l1b_ref[...])
        x1_ref[rows, :] = x1
        parts.append(x1 * (1.0 + sc2) + sh2)
    h2 = parts[0] if len(parts) == 1 else jnp.concatenate(parts, axis=0)
    h2t = h2.T.astype(BF16)
    h2t_ref[...] = h2t
    qpt_ref[...] = _dot(wpq_ref[...], h2t).astype(BF16)
    rows = PEER_HEADS * N_KEYS
    x1a = _dot(skbd_ref[0], qpt_ref[0:rows, :])
    x2a = _dot(skbd_ref[1], qpt_ref[rows:2 * rows, :])
    x1 = [x1a[k * PEER_HEADS:(k + 1) * PEER_HEADS, :] for k in range(N_KEYS)]
    x2 = [x2a[k * PEER_HEADS:(k + 1) * PEER_HEADS, :] for k in range(N_KEYS)]
    a, gap1 = _top_sorted(x1)
    b, gap2 = _top_sorted(x2)
    cand = [a[k1] + b[k2] for k1, cnt in enumerate(CAND_COUNTS) for k2 in range(cnt)]
    lowest = jnp.full((PEER_HEADS, tm), LOWEST, F32)
    top, gap3 = _top_sorted(cand + [lowest] * (CAND_ROWS - len(cand)))
    gap = jnp.minimum(jnp.minimum(gap1, gap2), gap3)
    thr = top[PEER_TOPK - 1]
    z = None
    n = []
    off = 0
    for k1, cnt in enumerate(CAND_COUNTS):
        nk = None
        for c in cand[off:off + cnt]:
            sel = c >= thr
            one = jnp.where(sel, 1.0, 0.0)
            nk = one if nk is None else nk + one
            e = jnp.where(sel, jnp.exp(c - cand[0]), 0.0)
            z = e if z is None else z + e
        n.append(nk)
        off += cnt
    zinv = 1.0 / z
    x1k = x1a.reshape(N_KEYS, PEER_HEADS, tm)
    x2k = x2a.reshape(N_KEYS, PEER_HEADS, tm)
    nrow = jnp.zeros_like(x1k)
    rank2 = jnp.full(x2k.shape, float(PEER_TOPK), F32)
    for r in range(PEER_TOPK):
        nrow = jnp.where(x1k == a[r][None], n[r][None], nrow)
        rank2 = jnp.where(x2k == b[r][None], float(r), rank2)

    blk = BF16_ROWS * PEER_HEADS
    for out_ref, v in ((rank2_ref, rank2), (e2_ref, jnp.exp(x2k - b[0][None])), (nrow_ref, nrow),
                       (e1z_ref, jnp.exp(x1k - a[0][None]) * zinv[None])):
        tab = v.reshape(rows, tm).astype(BF16)
        for kb in range(rows // blk):
            out_ref[kb * blk:(kb + 1) * blk, :] = _dot(perm_ref[...], tab[kb * blk:(kb + 1) * blk, :]).astype(BF16)

    @pl.when(jnp.min(gap) <= 0.0)
    def _():
        _route_by_extraction(qpt_ref, sk_ref, grp_ref, rank2_ref, e2_ref, nrow_ref, e1z_ref, cand_ref)


def _top_sorted(vals):
    x = list(vals)
    n = len(x)
    k = PEER_TOPK

    def order(i, j):
        hi = jnp.maximum(x[i], x[j])
        x[j] = jnp.minimum(x[i], x[j])
        x[i] = hi

    for base in range(0, n, k):
        size = 2
        while size <= k:
            j = size // 2
            while j >= 1:
                for i in range(k):
                    l = i ^ j
                    if l > i:
                        if i & size == 0:
                            order(base + i, base + l)
                        else:
                            order(base + l, base + i)
                j //= 2
            size *= 2
    dropped = None
    step = k
    while step < n:
        for base in range(0, n, 2 * step):
            for i in range(k):
                lo = jnp.minimum(x[base + i], x[base + step + k - 1 - i])
                x[base + i] = jnp.maximum(x[base + i], x[base + step + k - 1 - i])
                dropped = lo if dropped is None else jnp.maximum(dropped, lo)
            j = k // 2
            while j >= 1:
                for i in range(k):
                    l = i ^ j
                    if l > i:
                        order(base + i, base + l)
                j //= 2
        step *= 2
    top = x[:k]
    gap = top[k - 1] - dropped
    for r in range(k - 1):
        gap = jnp.minimum(gap, top[r] - top[r + 1])
    return top, gap


def _route_by_extraction(qpt_ref, sk_ref, grp_ref, rank2_ref, e2_ref, nrow_ref, e1z_ref, cand_ref):
    tm = qpt_ref.shape[1]
    half = D_KEY // 2
    pad0 = sum(CAND_COUNTS) // 8 * 8
    cand_ref[pad0:CAND_ROWS, :] = jnp.full((CAND_ROWS - pad0, tm), LOWEST, F32)
    side_rows = PEER_HEADS * half
    for hd in range(PEER_HEADS):
        s1 = _dot(sk_ref[2 * hd], qpt_ref[hd * half:(hd + 1) * half, :])
        s2 = _dot(sk_ref[2 * hd + 1], qpt_ref[side_rows + hd * half:side_rows + (hd + 1) * half, :])
        rank1, a = _extract_top(s1, PEER_TOPK)
        rank2, bvals = _extract_top(s2, PEER_TOPK)
        off = 0
        for k1, cnt in enumerate(CAND_COUNTS):
            cand_ref[off:off + cnt, :] = a[k1:k1 + 1, :] + bvals[0:cnt, :]
            off += cnt
        cand = cand_ref[...]
        crank, _ = _extract_top(cand, PEER_TOPK)
        sel = crank < float(PEER_TOPK)
        z = jnp.sum(jnp.where(sel, jnp.exp(cand - cand[0:1, :]), 0.0), axis=0, keepdims=True)
        n = _dot(grp_ref[...], jnp.where(sel, 1.0, 0.0).astype(BF16))
        nrow = jnp.zeros((N_KEYS, tm), F32)
        for k1 in range(PEER_TOPK):
            nrow = jnp.where(rank1 == float(k1), n[k1:k1 + 1, :], nrow)
        blk = BF16_ROWS * PEER_HEADS
        for out_ref, v in ((rank2_ref, rank2), (e2_ref, jnp.exp(s2 - bvals[0:1, :])), (nrow_ref, nrow),
                           (e1z_ref, jnp.exp(s1 - a[0:1, :]) / z)):
            vb = v.astype(BF16)
            for kb in range(N_KEYS // BF16_ROWS):
                out_ref[kb * blk + hd * BF16_ROWS:kb * blk + (hd + 1) * BF16_ROWS, :] = (
                    vb[kb * BF16_ROWS:(kb + 1) * BF16_ROWS, :])


def _route(x2d, an2d, bn2d, ln_g, ln_b, mod3, w, tm, seg, t_batch, alpha):
    n, d = x2d.shape
    assert n % tm == 0 and tm % seg == 0 and tm % LANES == 0 and PEER_HEADS == SUBLANES
    nseg = tm // seg
    rows = PEER_HEADS * N_KEYS
    tok = lambda width: pl.BlockSpec((tm, width), lambda i: (i, 0))
    full = lambda a: pl.BlockSpec(a.shape, lambda i: (0,) * a.ndim)
    tbl = pl.BlockSpec((rows, tm), lambda i: (0, i))
    weights = [w["w_o"], w["ln1_g"], w["ln1_b"], w["w_pq_t"], w["sub_keys"], w["sub_keys_bd"], w["key_perm"],
               w["cand_groups"]]
    return pl.pallas_call(
        functools.partial(_route_kernel, d_model=d, alpha=alpha, seg=seg),
        grid=(n // tm,),
        in_specs=[tok(d), tok(MLA_WIDTH), tok(GMLP_WIDTH), full(ln_g), full(ln_b),
                  pl.BlockSpec((nseg, 1, mod3.shape[2]), lambda i: ((i * tm // t_batch) // nseg, 0, 0))]
                 + [full(a) for a in weights],
        out_specs=[tok(d), pl.BlockSpec((d, tm), lambda i: (0, i)), tbl, tbl, tbl, tbl],
        out_shape=[jax.ShapeDtypeStruct((n, d), F32), jax.ShapeDtypeStruct((d, n), BF16)]
                  + [jax.ShapeDtypeStruct((rows, n), BF16)] * 4,
        scratch_shapes=[pltpu.VMEM((PEER_HEADS * D_KEY, tm), BF16), pltpu.VMEM((CAND_ROWS, tm), F32)],
        compiler_params=_cparams(("parallel",)),
        name="route",
    )(x2d, an2d, bn2d, ln_g, ln_b, mod3, *weights)


def _peer_kernel(h2t_ref, rank2_ref, e2_ref, nrow_ref, e1z_ref, u_ref, vt_ref, o_ref, a_ref, *, tw):
    c = pl.program_id(1)
    tb = h2t_ref.shape[1]
    rows_per_chunk = u_ref.shape[0] // N_KEYS
    cur = c % 2
    prev = 1 - cur

    @pl.when(c == 0)
    def _():
        o_ref[...] = jnp.zeros_like(o_ref)
        a_ref[1] = jnp.zeros(a_ref.shape[1:], BF16)

    def subtile(ts, carry):
        off = pl.multiple_of(ts * tw, tw)
        o_ref[:, pl.ds(off, tw)] += _dot(vt_ref[...], a_ref[prev, :, pl.ds(off, tw)])
        ht = h2t_ref[:, pl.ds(off, tw)]
        nr_rows = [nrow_ref[0, hd, :, pl.ds(off, tw)].astype(F32) for hd in range(PEER_HEADS)]
        e1_rows = [e1z_ref[0, hd, :, pl.ds(off, tw)].astype(F32) for hd in range(PEER_HEADS)]
        for p in range(rows_per_chunk):
            s = _dot(u_ref[p * N_KEYS:(p + 1) * N_KEYS, :], ht)
            wgt = jnp.zeros((N_KEYS // BF16_ROWS, BF16_ROWS, tw), BF16)
            for hd in range(PEER_HEADS):
                nr = jnp.broadcast_to(nr_rows[hd][p:p + 1, :], (BF16_ROWS, tw)).astype(BF16)
                e1 = jnp.broadcast_to(e1_rows[hd][p:p + 1, :], (BF16_ROWS, tw)).astype(BF16)
                r2 = rank2_ref[:, hd, :, pl.ds(off, tw)]
                e2 = e2_ref[:, hd, :, pl.ds(off, tw)]
                wgt = wgt + jnp.where(r2 < nr[None], e2, jnp.zeros_like(e2)) * e1[None]
            a_ref[cur, p * N_KEYS:(p + 1) * N_KEYS, pl.ds(off, tw)] = (
                _gelu(s).astype(BF16) * wgt.reshape(N_KEYS, tw))
        return carry

    lax.fori_loop(0, tb // tw, subtile, 0)


def _peer_mix(h2t, rank2, e2, nrow, e1z, u_b, vt_b, tb, tw, ec):
    d, n = h2t.shape
    ne = u_b.shape[0]
    assert n % tb == 0 and tb % tw == 0 and ne % ec == 0 and ec == BF16_ROWS * N_KEYS
    groups = N_KEYS // BF16_ROWS
    shape4 = (groups, PEER_HEADS, BF16_ROWS, n)
    rank2, e2, nrow, e1z = (a.reshape(shape4) for a in (rank2, e2, nrow, e1z))
    tbl = pl.BlockSpec((groups, PEER_HEADS, BF16_ROWS, tb), lambda b, c: (0, 0, 0, b))
    last = ne // ec - 1
    row = pl.BlockSpec((1, PEER_HEADS, BF16_ROWS, tb), lambda b, c: (jnp.minimum(c, last), 0, 0, b))
    return pl.pallas_call(
        functools.partial(_peer_kernel, tw=tw),
        grid=(n // tb, ne // ec + 1),
        in_specs=[pl.BlockSpec((d, tb), lambda b, c: (0, b)), tbl, tbl, row, row,
                  pl.BlockSpec((ec, d), lambda b, c: (jnp.minimum(c, last), 0)),
                  pl.BlockSpec((d, ec), lambda b, c: (0, jnp.maximum(c - 1, 0)))],
        out_specs=pl.BlockSpec((d, tb), lambda b, c: (0, b)),
        out_shape=jax.ShapeDtypeStruct((d, n), F32),
        scratch_shapes=[pltpu.VMEM((2, ec, tb), BF16)],
        compiler_params=_cparams(("parallel", "arbitrary")),
        name="peer_mix",
    )(h2t, rank2, e2, nrow, e1z, u_b, vt_b)


def _final_kernel(ft_ref, x1_ref, mod_ref, g_ref, b_ref, o_ref, *, d_model, alpha, seg):
    d = d_model
    tm = x1_ref.shape[0]
    f = ft_ref[...].T
    for sgm in range(tm // seg):
        rows = slice(sgm * seg, (sgm + 1) * seg)
        gt2 = mod_ref[sgm, :, 5 * d:6 * d]
        o_ref[rows, :] = _layer_norm(alpha * x1_ref[rows, :] + (1.0 + gt2) * f[rows], g_ref[...], b_ref[...])


def _final(ft, x1, mod3, g, b, tm, seg, t_batch, alpha):
    n, d = x1.shape
    nseg = tm // seg
    return pl.pallas_call(
        functools.partial(_final_kernel, d_model=d, alpha=alpha, seg=seg),
        grid=(n // tm,),
        in_specs=[pl.BlockSpec((d, tm), lambda i: (0, i)), pl.BlockSpec((tm, d), lambda i: (i, 0)),
                  pl.BlockSpec((nseg, 1, mod3.shape[2]), lambda i: ((i * tm // t_batch) // nseg, 0, 0)),
                  pl.BlockSpec(g.shape, lambda i: (0, 0)), pl.BlockSpec(b.shape, lambda i: (0, 0))],
        out_specs=pl.BlockSpec((tm, d), lambda i: (i, 0)),
        out_shape=jax.ShapeDtypeStruct((n, d), F32),
        compiler_params=_cparams(("parallel",)),
        name="final_norm",
    )(ft, x1, mod3, g, b)


def _rope_tables(past, t):
    half = QK_ROPE // 2
    inv_freq = jnp.power(jnp.float32(ROPE_BASE), -jnp.arange(half, dtype=F32) * (2.0 / QK_ROPE))
    pos = past + jnp.arange(t, dtype=jnp.int32)
    ang = pos.astype(F32)[:, None] * inv_freq[None, :]
    cos = jnp.cos(ang)
    sin = jnp.sin(ang)
    cos_t = jnp.tile(jnp.concatenate([cos, cos], axis=1), (1, MLA_HEADS))
    sin_t = jnp.tile(jnp.concatenate([-sin, sin], axis=1), (1, MLA_HEADS))
    return cos_t, sin_t


def _swap_halves(w):
    half = QK_ROPE // 2
    return jnp.concatenate([w[..., half:], w[..., :half]], axis=-1)


def _prepare_layer(w_in, g_q, g_kv, w_uq, w_uk, w_uv, g_v, b_v, w_s, b_s, g_out_a, g_out_b, w_o,
                   ln1_g, ln1_b, w_pq, sub_keys, peer_u, peer_v, ln2_g, ln2_b):
    o1 = Q_LORA
    o2 = o1 + KV_LORA
    o3 = o2 + QK_ROPE
    o4 = o3 + GMLP_WIDTH
    kr_w = w_in[:, o2:o3]
    kr_sw = _swap_halves(kr_w)
    w_in_x = jnp.concatenate([w_in[:, :o2], w_in[:, o3:o4], w_in[:, o4:], kr_w, kr_w, kr_sw, kr_sw], axis=1)
    uq = w_uq.reshape(Q_LORA, MLA_HEADS, QK_NOPE + QK_ROPE)
    uq_rope = uq[:, :, QK_NOPE:]
    w_uq_x = jnp.concatenate([uq[:, :, :QK_NOPE].reshape(Q_LORA, -1), uq_rope.reshape(Q_LORA, -1),
                              _swap_halves(uq_rope).reshape(Q_LORA, -1)], axis=1)
    row = lambda a: a.reshape(1, -1)
    groups = []
    for k1, cnt in enumerate(CAND_COUNTS):
        groups += [k1] * cnt
    groups += [-1] * (CAND_ROWS - len(groups))
    cand_groups = (jnp.arange(PEER_TOPK)[:, None] == jnp.asarray(groups)[None, :]).astype(BF16)
    half = D_KEY // 2
    w_pq_t = w_pq.reshape(-1, PEER_HEADS, 2, half).transpose(2, 1, 3, 0).reshape(2 * PEER_HEADS * half, -1)
    sub_keys_bd = jnp.einsum("hpkd,hg->pkhgd", sub_keys, jnp.eye(PEER_HEADS, dtype=sub_keys.dtype))
    sub_keys_bd = sub_keys_bd.reshape(2, N_KEYS * PEER_HEADS, PEER_HEADS * half)
    src = jnp.arange(BF16_ROWS * PEER_HEADS)
    key_perm = ((src % BF16_ROWS) * PEER_HEADS + src // BF16_ROWS)[:, None] == src[None, :]
    return dict(
        sub_keys_bd=sub_keys_bd.astype(BF16), key_perm=key_perm.astype(BF16),
        w_in=w_in_x.astype(BF16), g_q=row(g_q), g_kv=row(g_kv), w_uq=w_uq_x.astype(BF16),
        w_ukv=jnp.concatenate([w_uk, w_uv], axis=1).astype(BF16), g_v=row(g_v), b_v=row(b_v), w_s=w_s,
        b_s=jnp.broadcast_to(b_s[:, :, None], b_s.shape + (GMLP_HEAD_DIM,)), g_out_a=row(g_out_a),
        g_out_b=row(g_out_b), w_o=w_o.astype(BF16), ln1_g=row(ln1_g), ln1_b=row(ln1_b),
        w_pq_t=w_pq_t.astype(BF16), sub_keys=sub_keys.reshape(PEER_HEADS * 2, N_KEYS, D_KEY // 2).astype(BF16),
        cand_groups=cand_groups, peer_u=peer_u.astype(BF16), peer_vt=peer_v.T.astype(BF16),
        ln2_g=row(ln2_g), ln2_b=row(ln2_b),
        kr_dup=jnp.concatenate([jnp.eye(QK_ROPE, dtype=BF16)] * 2, axis=1),
    )


def _pick_tile(n, prefs):
    for p in prefs:
        if n % p == 0:
            return p
    raise ValueError(f"no tile for {n}")


def _layer_group(x, mod, cache_ckv, cache_kr, w, ln_g, ln_b, alpha):
    b, t, d = x.shape
    n = b * t
    past = 0 if cache_ckv is None else cache_ckv.shape[1]
    mod3 = mod.reshape(b, 1, mod.shape[1])
    cos, sin = _rope_tables(past, t)

    tm = _pick_tile(t, (256, 128)) if t >= GMLP_CHUNK else t
    outs = _input_projection(x, ln_g, ln_b, mod3, w, cos, sin, tm, cache_ckv is not None)
    q, k_new, v_new, ckv, kr, bn = outs[:6]
    v_rows = outs[6] if cache_ckv is not None else None

    tq = _pick_tile(t, (512, 256, 128, 64))
    if cache_ckv is None:
        tk = _pick_tile(t, (512, 256))
        k_all, v_all, n_valid = k_new, v_new, t
    else:
        k_c, v_c = _cache_projection(cache_ckv, cache_kr, w["w_ukv"], w["kr_dup"], _pick_tile(past, (512, 256, 128)))
        n_valid = past + t
        pad = (-n_valid) % 256
        tk = _pick_tile(n_valid + pad, (768, 512, 256))
        k_all = jnp.concatenate([k_c, k_new, jnp.zeros((b, pad, k_new.shape[2]), BF16)], axis=1)
        v_all = jnp.concatenate([v_c, v_new, jnp.zeros((b, v_new.shape[1], pad), BF16)], axis=2)
    an = _attention(q, k_all, v_all, w["g_out_a"], past, n_valid, tq, tk)

    tr = _pick_tile(n, (256, 128))
    seg = min(t, tr)
    assert tr % seg == 0 and t % seg == 0
    x1, h2t, rank2, e2, nrow, e1z = _route(x.reshape(n, d), an.reshape(n, -1), bn.reshape(n, -1), ln_g, ln_b,
                                          mod3, w, tr, seg, t, alpha)
    tb = _pick_tile(n, (512, 256))
    ft = _peer_mix(h2t, rank2, e2, nrow, e1z, w["peer_u"], w["peer_vt"], tb, 256, BF16_ROWS * N_KEYS)
    y = _final(ft, x1, mod3, w["ln2_g"], w["ln2_b"], tr, seg, t, alpha)
    return y.reshape(b, t, d), ckv, kr, v_rows


def kernel(x_prompt, x_sample, cache_ckv, cache_krope, c_prompt, c_sample, ln_in_g, ln_in_b, w_ada, b_ada, w_in,
           g_q, g_kv, w_uq, w_uk, w_uv, g_v, b_v, w_s, b_s, g_out_a, g_out_b, w_o, ln1_g, ln1_b, w_pq, sub_keys,
           peer_u, peer_v, ln2_g, ln2_b):
    depth = w_ada.shape[0]
    assert depth == 1, "the entry LayerNorm is fused into the first layer's kernels"
    alpha = (2 * depth) ** 0.25
    bp = c_prompt.shape[0]
    bs = c_sample.shape[0]
    c_all = jnp.concatenate([c_prompt, c_sample], axis=0)
    c_all = jnp.pad(c_all, ((0, (-c_all.shape[0]) % 8), (0, 0)))
    ln_g = ln_in_g.reshape(1, -1)
    ln_b = ln_in_b.reshape(1, -1)
    hp, hs = x_prompt, x_sample
    ckv_p, kr_p, ckv_s, kr_s, gv_s = [], [], [], [], []
    for l in range(depth):
        w = _prepare_layer(w_in[l], g_q[l], g_kv[l], w_uq[l], w_uk[l], w_uv[l], g_v[l], b_v[l], w_s[l], b_s[l],
                           g_out_a[l], g_out_b[l], w_o[l], ln1_g[l], ln1_b[l], w_pq[l], sub_keys[l], peer_u[l],
                           peer_v[l], ln2_g[l], ln2_b[l])
        mod = _ada_mod(c_all, w_ada[l], b_ada[l])
        hp, ckv_new_p, kr_new_p, _ = _layer_group(hp, mod[:bp], None, None, w, ln_g, ln_b, alpha)
        hs, ckv_new_s, kr_new_s, v_new_s = _layer_group(hs, mod[bp:bp + bs], cache_ckv[l], cache_krope[l], w,
                                                       ln_g, ln_b, alpha)
        ckv_p.append(ckv_new_p)
        kr_p.append(kr_new_p)
        ckv_s.append(ckv_new_s)
        kr_s.append(kr_new_s)
        gv_s.append(v_new_s)
    return (hp, hs, jnp.stack(ckv_p), jnp.stack(kr_p), jnp.stack(ckv_s), jnp.stack(kr_s), jnp.stack(gv_s))
```

```python
import functools
import math

import jax
import jax.numpy as jnp
from jax import lax
from jax.experimental import pallas as pl
from jax.experimental.pallas import tpu as pltpu

F32 = jnp.float32
BF16 = jnp.bfloat16

CHUNK = 64
MLA_HEADS = 4
QK_NOPE = 128
QK_ROPE = 64
V_HEAD = 128
Q_LORA = 768
KV_LORA = 256
ROPE_BASE = 10000.0
HEAD_PAD = 2 * QK_NOPE
GMLP_HEADS = 4
GMLP_HEAD_DIM = 128
GMLP_CHUNK = 128
MLA_WIDTH = MLA_HEADS * V_HEAD
GMLP_WIDTH = GMLP_HEADS * GMLP_HEAD_DIM
PEER_HEADS = 8
N_KEYS = 128
PEER_TOPK = 16
D_KEY = 256
EPS = 1e-6
NEG = -1e30
LOWEST = -3.0e38

LANES = 128
SUBLANES = 8
BF16_ROWS = 16
VMEM_LIMIT = 56 * 1024 * 1024

CAND_COUNTS = tuple(PEER_TOPK // (k + 1) for k in range(PEER_TOPK))
CAND_ROWS = 64


def _cparams(sem):
    return pltpu.CompilerParams(dimension_semantics=sem, vmem_limit_bytes=VMEM_LIMIT)


def _gelu(x):
    c1 = -2.0 * math.sqrt(2.0 / math.pi) * math.log2(math.e)
    return x / (1.0 + jnp.exp2(x * (c1 + (c1 * 0.044715) * (x * x))))


def _layer_norm(x, g, b):
    mu = jnp.mean(x, axis=-1, keepdims=True)
    xc = x - mu
    var = jnp.mean(xc * xc, axis=-1, keepdims=True)
    return xc * lax.rsqrt(var + EPS) * g + b


def _rms_norm(x, g):
    ms = jnp.mean(x * x, axis=-1, keepdims=True)
    return x * lax.rsqrt(ms + EPS) * g


def _dot(a, b):
    return jnp.dot(a, b, preferred_element_type=F32)


def _transpose(x):
    return x.T


def _dot_nt(a, b):
    return lax.dot_general(a, b, (((1,), (1,)), ((), ())), preferred_element_type=F32)


def _ada_kernel(c_ref, w_ref, b_ref, o_ref):
    c = c_ref[...]
    s = c * (1.0 / (1.0 + jnp.exp(-c)))
    o_ref[...] = _dot(s.astype(BF16), w_ref[...].astype(BF16)) + b_ref[...]


def _ada_mod(c, w_ada, b_ada):
    bc, d = c.shape
    n = w_ada.shape[1]
    tn = 1536
    return pl.pallas_call(
        _ada_kernel,
        grid=(n // tn,),
        in_specs=[pl.BlockSpec((bc, d), lambda j: (0, 0)),
                  pl.BlockSpec((d, tn), lambda j: (0, j)),
                  pl.BlockSpec((1, tn), lambda j: (0, j))],
        out_specs=pl.BlockSpec((bc, tn), lambda j: (0, j)),
        out_shape=jax.ShapeDtypeStruct((bc, n), F32),
        compiler_params=_cparams(("arbitrary",)),
        name="ada_mod",
    )(c, w_ada, b_ada.reshape(1, n))


def _proj_kernel(x_ref, lng_ref, lnb_ref, mod_ref, win_ref, gq_ref, gkv_ref, wuq_ref, wukv_ref,
                 gv_ref, bv_ref, ws_ref, bs_ref, gob_ref, cos_ref, sin_ref,
                 q_ref, k_ref, vt_ref, ckv_ref, kr_ref, bn_ref, *vrow_refs, d_model):
    d = d_model
    tm = x_ref.shape[0]
    h0 = _layer_norm(x_ref[...], lng_ref[...], lnb_ref[...])
    sh1 = mod_ref[:, 0:d]
    sc1 = mod_ref[:, d:2 * d]
    h = h0 * (1.0 + sc1) + sh1
    z = _dot(h.astype(BF16), win_ref[...])
    o1 = Q_LORA
    o2 = o1 + KV_LORA
    o3 = o2 + GMLP_WIDTH
    o4 = o3 + GMLP_WIDTH
    o5 = o4 + LANES
    cos = cos_ref[...]
    sin = sin_ref[...]

    cq = _rms_norm(z[:, 0:o1], gq_ref[...])
    qa = _dot(cq.astype(BF16), wuq_ref[...])
    nw = MLA_HEADS * QK_NOPE
    rw = MLA_HEADS * QK_ROPE
    scale = math.log2(math.e) / math.sqrt(QK_NOPE + QK_ROPE)
    qr = (qa[:, nw:nw + rw] * cos + qa[:, nw + rw:nw + 2 * rw] * sin) * scale
    lane = lax.broadcasted_iota(jnp.int32, (tm, LANES), 1)
    for hd in range(MLA_HEADS):
        base = hd * HEAD_PAD
        q_ref[:, base:base + QK_NOPE] = (qa[:, hd * QK_NOPE:(hd + 1) * QK_NOPE] * scale).astype(BF16)
        tile = qr[:, (hd // 2) * LANES:(hd // 2 + 1) * LANES]
        keep = (lane < QK_ROPE) if hd % 2 == 0 else (lane >= QK_ROPE)
        q_ref[:, base + QK_NOPE:base + HEAD_PAD] = jnp.where(keep, tile, 0.0).astype(BF16)

    ckv = _rms_norm(z[:, o1:o2], gkv_ref[...])
    ckv_ref[...] = ckv
    kv = _dot(ckv.astype(BF16), wukv_ref[...])
    krd = z[:, o4:o5] * cos[:, 0:LANES] + z[:, o5:o5 + LANES] * sin[:, 0:LANES]
    kr_ref[...] = krd[:, 0:QK_ROPE]
    krd_b = krd.astype(BF16)
    for hd in range(MLA_HEADS):
        base = hd * HEAD_PAD
        k_ref[:, base:base + QK_NOPE] = kv[:, hd * QK_NOPE:(hd + 1) * QK_NOPE].astype(BF16)
        k_ref[:, base + QK_NOPE:base + HEAD_PAD] = krd_b
    vt_ref[...] = _transpose(kv[:, nw:nw + MLA_WIDTH]).astype(BF16)

    u = _gelu(z[:, o2:o3])
    v = _layer_norm(_gelu(z[:, o3:o4]), gv_ref[...], bv_ref[...])
    if vrow_refs:
        vrow_refs[0][...] = v
    vb = v.astype(BF16)
    cr = min(tm, GMLP_CHUNK)
    ri = lax.broadcasted_iota(jnp.int32, (cr, cr), 0) // CHUNK
    ci = lax.broadcasted_iota(jnp.int32, (cr, cr), 1) // CHUNK
    chunks = []
    for c in range(tm // cr):
        heads = []
        for hd in range(GMLP_HEADS):
            w = jnp.where(ri >= ci, ws_ref[hd, 0:cr, 0:cr], 0.0).astype(BF16)
            cols = slice(hd * GMLP_HEAD_DIM, (hd + 1) * GMLP_HEAD_DIM)
            mixed = _dot(w, vb[c * cr:(c + 1) * cr, cols]) + bs_ref[hd, 0:cr, :]
            heads.append(u[c * cr:(c + 1) * cr, cols] * mixed)
        chunks.append(jnp.concatenate(heads, axis=1))
    b_out = chunks[0] if len(chunks) == 1 else jnp.concatenate(chunks, axis=0)
    bn_ref[...] = _rms_norm(b_out, gob_ref[...]).astype(BF16)


def _input_projection(x, ln_g, ln_b, mod3, w, cos, sin, tm, with_vrows):
    b, t, d = x.shape
    assert t % tm == 0 and (tm % GMLP_CHUNK == 0 or (tm == t and t < GMLP_CHUNK))
    full = lambda a: pl.BlockSpec(a.shape, lambda bi, i: (0,) * a.ndim)
    tok = lambda width: pl.BlockSpec((None, tm, width), lambda bi, i: (bi, i, 0))
    weights = [w["w_in"], w["g_q"], w["g_kv"], w["w_uq"], w["w_ukv"], w["g_v"], w["b_v"], w["w_s"], w["b_s"],
               w["g_out_b"]]
    out_widths = [(MLA_HEADS * HEAD_PAD, BF16), (MLA_HEADS * HEAD_PAD, BF16), None,
                  (KV_LORA, F32), (QK_ROPE, F32), (GMLP_WIDTH, BF16)]
    if with_vrows:
        out_widths.append((GMLP_WIDTH, F32))
    out_specs = [pl.BlockSpec((None, MLA_WIDTH, tm), lambda bi, i: (bi, 0, i)) if o is None else tok(o[0])
                 for o in out_widths]
    out_shape = [jax.ShapeDtypeStruct((b, MLA_WIDTH, t), BF16) if o is None
                 else jax.ShapeDtypeStruct((b, t, o[0]), o[1]) for o in out_widths]
    return pl.pallas_call(
        functools.partial(_proj_kernel, d_model=d),
        grid=(b, t // tm),
        in_specs=[tok(d), full(ln_g), full(ln_b),
                  pl.BlockSpec((None, 1, mod3.shape[2]), lambda bi, i: (bi, 0, 0))]
                 + [full(a) for a in weights]
                 + [pl.BlockSpec((tm, cos.shape[1]), lambda bi, i: (i, 0))] * 2,
        out_specs=out_specs,
        out_shape=out_shape,
        compiler_params=_cparams(("parallel", "parallel")),
        name="input_projection",
    )(x, ln_g, ln_b, mod3, *weights, cos, sin)


def _cache_kernel(ckv_ref, kr_ref, wukv_ref, dup_ref, k_ref, vt_ref):
    kv = _dot(ckv_ref[...].astype(BF16), wukv_ref[...])
    krd = _dot(kr_ref[...].astype(BF16), dup_ref[...]).astype(BF16)
    nw = MLA_HEADS * QK_NOPE
    for hd in range(MLA_HEADS):
        base = hd * HEAD_PAD
        k_ref[:, base:base + QK_NOPE] = kv[:, hd * QK_NOPE:(hd + 1) * QK_NOPE].astype(BF16)
        k_ref[:, base + QK_NOPE:base + HEAD_PAD] = krd
    vt_ref[...] = _transpose(kv[:, nw:nw + MLA_WIDTH]).astype(BF16)


def _cache_projection(cache_ckv, cache_kr, w_ukv, dup, tm):
    b, p, _ = cache_ckv.shape
    assert p % tm == 0
    tok = lambda width: pl.BlockSpec((None, tm, width), lambda bi, i: (bi, i, 0))
    full = lambda a: pl.BlockSpec(a.shape, lambda bi, i: (0,) * a.ndim)
    return pl.pallas_call(
        _cache_kernel,
        grid=(b, p // tm),
        in_specs=[tok(KV_LORA), tok(QK_ROPE), full(w_ukv), full(dup)],
        out_specs=[tok(MLA_HEADS * HEAD_PAD), pl.BlockSpec((None, MLA_WIDTH, tm), lambda bi, i: (bi, 0, i))],
        out_shape=[jax.ShapeDtypeStruct((b, p, MLA_HEADS * HEAD_PAD), BF16),
                   jax.ShapeDtypeStruct((b, MLA_WIDTH, p), BF16)],
        compiler_params=_cparams(("parallel", "parallel")),
        name="cache_projection",
    )(cache_ckv, cache_kr, w_ukv, dup)


def _attn_kernel(q_ref, k_ref, vt_ref, ga_ref, o_ref, *, past, n_valid, tq, tk):
    i = pl.program_id(1)
    q_first = past + i * tq
    lo = jnp.minimum((q_first // CHUNK + 1) * CHUNK, n_valid)
    hi = jnp.minimum(((q_first + tq - 1) // CHUNK + 1) * CHUNK, n_valid)
    n_full = lo // tk
    n_tiles = (hi + tk - 1) // tk
    q_chunk = (q_first + lax.broadcasted_iota(jnp.int32, (tk, tq), 1)) // CHUNK
    k_iota = lax.broadcasted_iota(jnp.int32, (tk, tq), 0)

    def make_step(masked):
        def step(j, carry):
            off = pl.multiple_of(j * tk, tk)
            if masked:
                k_pos = off + k_iota
                visible = (k_pos // CHUNK <= q_chunk) & (k_pos < n_valid)
            heads = range(MLA_HEADS)
            scores = [_dot_nt(k_ref[pl.ds(off, tk), hd * HEAD_PAD:(hd + 1) * HEAD_PAD],
                              q_ref[:, hd * HEAD_PAD:(hd + 1) * HEAD_PAD]) for hd in heads]
            probs, stats = [], []
            for hd in heads:
                m, l, _ = carry[hd]
                s = scores[hd]
                if masked:
                    s = jnp.where(visible, s, NEG)
                m_new = jnp.maximum(m, jnp.max(s, axis=0, keepdims=True))
                alpha = jnp.exp2(m - m_new)
                p = jnp.exp2(s - m_new)
                stats.append((m_new, alpha * l + jnp.sum(p, axis=0, keepdims=True), alpha))
                probs.append(p.astype(BF16))
            new = []
            for hd in heads:
                m_new, l, alpha = stats[hd]
                vth = vt_ref[hd * V_HEAD:(hd + 1) * V_HEAD, pl.ds(off, tk)]
                new.append((m_new, l, alpha * carry[hd][2] + _dot(vth, probs[hd])))
            return tuple(new)
        return step

    init = tuple((jnp.full((1, tq), NEG, F32), jnp.zeros((1, tq), F32), jnp.zeros((V_HEAD, tq), F32))
                 for _ in range(MLA_HEADS))
    carry = lax.fori_loop(0, n_full, make_step(False), init)
    carry = lax.fori_loop(n_full, n_tiles, make_step(True), carry)
    a_t = jnp.concatenate([acc / l for _, l, acc in carry], axis=0)
    o_ref[...] = _rms_norm(_transpose(a_t), ga_ref[...]).astype(BF16)


def _attention(q, k, vt, g_out_a, past, n_valid, tq, tk):
    b, t, _ = q.shape
    lk = k.shape[1]
    assert t % tq == 0 and lk % tk == 0 and n_valid <= lk and vt.shape[2] == lk
    return pl.pallas_call(
        functools.partial(_attn_kernel, past=past, n_valid=n_valid, tq=tq, tk=tk),
        grid=(b, t // tq),
        in_specs=[pl.BlockSpec((None, tq, q.shape[2]), lambda bi, i: (bi, i, 0)),
                  pl.BlockSpec((None, lk, k.shape[2]), lambda bi, i: (bi, 0, 0)),
                  pl.BlockSpec((None, vt.shape[1], lk), lambda bi, i: (bi, 0, 0)),
                  pl.BlockSpec(g_out_a.shape, lambda bi, i: (0, 0))],
        out_specs=pl.BlockSpec((None, tq, MLA_WIDTH), lambda bi, i: (bi, i, 0)),
        out_shape=jax.ShapeDtypeStruct((b, t, MLA_WIDTH), BF16),
        compiler_params=_cparams(("parallel", "arbitrary")),
        name="attention",
    )(q, k, vt, g_out_a)


def _extract_top(s, rounds):
    r_, l_ = s.shape
    ridx = lax.broadcasted_iota(jnp.int32, (r_, l_), 0).astype(F32)
    vidx = lax.broadcasted_iota(jnp.int32, (rounds, l_), 0)
    rank = jnp.full((r_, l_), float(rounds), F32)
    vals = jnp.zeros((rounds, l_), F32)
    for r in range(rounds):
        m = jnp.max(s, axis=0, keepdims=True)
        first = jnp.min(jnp.where(s == m, ridx, float(r_)), axis=0, keepdims=True)
        hit = ridx == first
        rank = jnp.where(hit, float(r), rank)
        s = jnp.where(hit, LOWEST, s)
        vals = jnp.where(vidx == r, m, vals)
    return rank, vals


def _route_kernel(x_ref, an_ref, bn_ref, lng_ref, lnb_ref, mod_ref, wo_ref, l1g_ref, l1b_ref, wpq_ref, sk_ref,
                  skbd_ref, perm_ref, grp_ref, x1_ref, h2t_ref, rank2_ref, e2_ref, nrow_ref, e1z_ref, qpt_ref,
                  cand_ref, *, d_model, alpha, seg):
    d = d_model
    tm = x_ref.shape[0]
    h0 = _layer_norm(x_ref[...], lng_ref[...], lnb_ref[...])
    mix = _dot(an_ref[...], wo_ref[0:MLA_WIDTH, :]) + _dot(bn_ref[...], wo_ref[MLA_WIDTH:, :])
    parts = []
    for sgm in range(tm // seg):
        rows = slice(sgm * seg, (sgm + 1) * seg)
        gt1 = mod_ref[sgm, :, 2 * d:3 * d]
        sh2 = mod_ref[sgm, :, 3 * d:4 * d]
        sc2 = mod_ref[sgm, :, 4 * d:5 * d]
        x1 = _layer_norm(alpha * h0[rows] + (1.0 + gt1) * mix[rows], l1g_ref[...], l1b_ref[...])
        x1_ref[rows, :] = x1
        parts.append(x1 * (1.0 + sc2) + sh2)
    h2 = parts[0] if len(parts) == 1 else jnp.concatenate(parts, axis=0)
    h2t = h2.T.astype(BF16)
    h2t_ref[...] = h2t
    qpt_ref[...] = _dot(wpq_ref[...], h2t).astype(BF16)
    rows = PEER_HEADS * N_KEYS
    x1a = _dot(skbd_ref[0], qpt_ref[0:rows, :])
    x2a = _dot(skbd_ref[1], qpt_ref[rows:2 * rows, :])
    x1 = [x1a[k * PEER_HEADS:(k + 1) * PEER_HEADS, :] for k in range(N_KEYS)]
    x2 = [x2a[k * PEER_HEADS:(k + 1) * PEER_HEADS, :] for k in range(N_KEYS)]
    a, gap1 = _top_sorted(x1)
    b, gap2 = _top_sorted(x2)
    cand = [a[k1] + b[k2] for k1, cnt in enumerate(CAND_COUNTS) for k2 in range(cnt)]
    lowest = jnp.full((PEER_HEADS, tm), LOWEST, F32)
    top, gap3 = _top_sorted(cand + [lowest] * (CAND_ROWS - len(cand)))
    gap = jnp.minimum(jnp.minimum(gap1, gap2), gap3)
    thr = top[PEER_TOPK - 1]
    z = None
    n = []
    off = 0
    for k1, cnt in enumerate(CAND_COUNTS):
        nk = None
        for c in cand[off:off + cnt]:
            sel = c >= thr
            one = jnp.where(sel, 1.0, 0.0)
            nk = one if nk is None else nk + one
            e = jnp.where(sel, jnp.exp(c - cand[0]), 0.0)
            z = e if z is None else z + e
        n.append(nk)
        off += cnt
    zinv = 1.0 / z
    x1k = x1a.reshape(N_KEYS, PEER_HEADS, tm)
    x2k = x2a.reshape(N_KEYS, PEER_HEADS, tm)
    nrow = jnp.zeros_like(x1k)
    rank2 = jnp.full(x2k.shape, float(PEER_TOPK), F32)
    for r in range(PEER_TOPK):
        nrow = jnp.where(x1k == a[r][None], n[r][None], nrow)
        rank2 = jnp.where(x2k == b[r][None], float(r), rank2)

    blk = BF16_ROWS * PEER_HEADS
    for out_ref, v in ((rank2_ref, rank2), (e2_ref, jnp.exp(x2k - b[0][None])), (nrow_ref, nrow),
                       (e1z_ref, jnp.exp(x1k - a[0][None]) * zinv[None])):
        tab = v.reshape(rows, tm).astype(BF16)
        for kb in range(rows // blk):
            out_ref[kb * blk:(kb + 1) * blk, :] = _dot(perm_ref[...], tab[kb * blk:(kb + 1) * blk, :]).astype(BF16)

    pad0 = sum(CAND_COUNTS) // SUBLANES * SUBLANES
    cand_ref[pad0:CAND_ROWS, :] = jnp.full((CAND_ROWS - pad0, tm), LOWEST, F32)
    for hd in range(PEER_HEADS):
        @pl.when(jnp.min(gap[hd:hd + 1, :]) <= 0.0)
        def _(hd=hd):
            _route_head_by_extraction(hd, qpt_ref, sk_ref, grp_ref, rank2_ref, e2_ref, nrow_ref, e1z_ref, cand_ref)


def _top_sorted(vals):
    x = list(vals)
    n = len(x)
    k = PEER_TOPK

    def order(i, j):
        hi = jnp.maximum(x[i], x[j])
        x[j] = jnp.minimum(x[i], x[j])
        x[i] = hi

    for base in range(0, n, k):
        size = 2
        while size <= k:
            j = size // 2
            while j >= 1:
                for i in range(k):
                    l = i ^ j
                    if l > i:
                        if i & size == 0:
                            order(base + i, base + l)
                        else:
                            order(base + l, base + i)
                j //= 2
            size *= 2
    dropped = None
    step = k
    while step < n:
        for base in range(0, n, 2 * step):
            for i in range(k):
                lo = jnp.minimum(x[base + i], x[base + step + k - 1 - i])
                x[base + i] = jnp.maximum(x[base + i], x[base + step + k - 1 - i])
                dropped = lo if dropped is None else jnp.maximum(dropped, lo)
            j = k // 2
            while j >= 1:
                for i in range(k):
                    l = i ^ j
                    if l > i:
                        order(base + i, base + l)
                j //= 2
        step *= 2
    top = x[:k]
    gap = top[k - 1] - dropped
    for r in range(k - 1):
        gap = jnp.minimum(gap, top[r] - top[r + 1])
    return top, gap


def _route_head_by_extraction(hd, qpt_ref, sk_ref, grp_ref, rank2_ref, e2_ref, nrow_ref, e1z_ref, cand_ref):
    tm = qpt_ref.shape[1]
    half = D_KEY // 2
    side_rows = PEER_HEADS * half
    s1 = _dot(sk_ref[2 * hd], qpt_ref[hd * half:(hd + 1) * half, :])
    s2 = _dot(sk_ref[2 * hd + 1], qpt_ref[side_rows + hd * half:side_rows + (hd + 1) * half, :])
    rank1, a = _extract_top(s1, PEER_TOPK)
    rank2, bvals = _extract_top(s2, PEER_TOPK)
    off = 0
    for k1, cnt in enumerate(CAND_COUNTS):
        cand_ref[off:off + cnt, :] = a[k1:k1 + 1, :] + bvals[0:cnt, :]
        off += cnt
    cand = cand_ref[...]
    crank, _ = _extract_top(cand, PEER_TOPK)
    sel = crank < float(PEER_TOPK)
    z = jnp.sum(jnp.where(sel, jnp.exp(cand - cand[0:1, :]), 0.0), axis=0, keepdims=True)
    n = _dot(grp_ref[...], jnp.where(sel, 1.0, 0.0).astype(BF16))
    nrow = jnp.zeros((N_KEYS, tm), F32)
    for k1 in range(PEER_TOPK):
        nrow = jnp.where(rank1 == float(k1), n[k1:k1 + 1, :], nrow)
    blk = BF16_ROWS * PEER_HEADS
    for out_ref, v in ((rank2_ref, rank2), (e2_ref, jnp.exp(s2 - bvals[0:1, :])), (nrow_ref, nrow),
                       (e1z_ref, jnp.exp(s1 - a[0:1, :]) / z)):
        vb = v.astype(BF16)
        for kb in range(N_KEYS // BF16_ROWS):
            out_ref[kb * blk + hd * BF16_ROWS:kb * blk + (hd + 1) * BF16_ROWS, :] = (
                vb[kb * BF16_ROWS:(kb + 1) * BF16_ROWS, :])


def _route(x2d, an2d, bn2d, ln_g, ln_b, mod3, w, tm, seg, t_batch, alpha):
    n, d = x2d.shape
    assert n % tm == 0 and tm % seg == 0 and tm % LANES == 0 and PEER_HEADS == SUBLANES
    nseg = tm // seg
    rows = PEER_HEADS * N_KEYS
    tok = lambda width: pl.BlockSpec((tm, width), lambda i: (i, 0))
    full = lambda a: pl.BlockSpec(a.shape, lambda i: (0,) * a.ndim)
    tbl = pl.BlockSpec((rows, tm), lambda i: (0, i))
    weights = [w["w_o"], w["ln1_g"], w["ln1_b"], w["w_pq_t"], w["sub_keys"], w["sub_keys_bd"], w["key_perm"],
               w["cand_groups"]]
    return pl.pallas_call(
        functools.partial(_route_kernel, d_model=d, alpha=alpha, seg=seg),
        grid=(n // tm,),
        in_specs=[tok(d), tok(MLA_WIDTH), tok(GMLP_WIDTH), full(ln_g), full(ln_b),
                  pl.BlockSpec((nseg, 1, mod3.shape[2]), lambda i: ((i * tm // t_batch) // nseg, 0, 0))]
                 + [full(a) for a in weights],
        out_specs=[tok(d), pl.BlockSpec((d, tm), lambda i: (0, i)), tbl, tbl, tbl, tbl],
        out_shape=[jax.ShapeDtypeStruct((n, d), F32), jax.ShapeDtypeStruct((d, n), BF16)]
                  + [jax.ShapeDtypeStruct((rows, n), BF16)] * 4,
        scratch_shapes=[pltpu.VMEM((PEER_HEADS * D_KEY, tm), BF16), pltpu.VMEM((CAND_ROWS, tm), F32)],
        compiler_params=_cparams(("parallel",)),
        name="route",
    )(x2d, an2d, bn2d, ln_g, ln_b, mod3, *weights)


def _peer_kernel(h2t_ref, rank2_ref, e2_ref, nrow_ref, e1z_ref, u_ref, vt_ref, o_ref, a_ref, *, tw, n_chunks):
    s = pl.program_id(0)
    tb = h2t_ref.shape[1]
    rows_per_chunk = u_ref.shape[0] // N_KEYS
    cur = s % 2
    prev = 1 - cur

    @pl.when(s == 0)
    def _():
        a_ref[1] = jnp.zeros(a_ref.shape[1:], BF16)

    @pl.when(jnp.maximum(s - 1, 0) % n_chunks == 0)
    def _():
        o_ref[...] = jnp.zeros_like(o_ref)

    def subtile(ts, carry):
        off = pl.multiple_of(ts * tw, tw)
        o_ref[:, pl.ds(off, tw)] += _dot(vt_ref[...], a_ref[prev, :, pl.ds(off, tw)])
        ht = h2t_ref[:, pl.ds(off, tw)]
        nr_rows = [nrow_ref[0, hd, :, pl.ds(off, tw)].astype(F32) for hd in range(PEER_HEADS)]
        e1_rows = [e1z_ref[0, hd, :, pl.ds(off, tw)].astype(F32) for hd in range(PEER_HEADS)]
        for p in range(rows_per_chunk):
            s = _dot(u_ref[p * N_KEYS:(p + 1) * N_KEYS, :], ht)
            wgt = jnp.zeros((N_KEYS // BF16_ROWS, BF16_ROWS, tw), BF16)
            for hd in range(PEER_HEADS):
                nr = jnp.broadcast_to(nr_rows[hd][p:p + 1, :], (BF16_ROWS, tw)).astype(BF16)
                e1 = jnp.broadcast_to(e1_rows[hd][p:p + 1, :], (BF16_ROWS, tw)).astype(BF16)
                r2 = rank2_ref[:, hd, :, pl.ds(off, tw)]
                e2 = e2_ref[:, hd, :, pl.ds(off, tw)]
                wgt = wgt + jnp.where(r2 < nr[None], e2, jnp.zeros_like(e2)) * e1[None]
            a_ref[cur, p * N_KEYS:(p + 1) * N_KEYS, pl.ds(off, tw)] = (
                _gelu(s).astype(BF16) * wgt.reshape(N_KEYS, tw))
        return carry

    lax.fori_loop(0, tb // tw, subtile, 0)


def _peer_mix(h2t, rank2, e2, nrow, e1z, u_b, vt_b, tb, tw, ec):
    d, n = h2t.shape
    ne = u_b.shape[0]
    assert n % tb == 0 and tb % tw == 0 and ne % ec == 0 and ec == BF16_ROWS * N_KEYS
    groups = N_KEYS // BF16_ROWS
    shape4 = (groups, PEER_HEADS, BF16_ROWS, n)
    rank2, e2, nrow, e1z = (a.reshape(shape4) for a in (rank2, e2, nrow, e1z))
    nc = ne // ec
    nb = n // tb
    blk_in = lambda s: jnp.minimum(s // nc, nb - 1)
    blk_out = lambda s: jnp.maximum(s - 1, 0) // nc
    tbl = pl.BlockSpec((groups, PEER_HEADS, BF16_ROWS, tb), lambda s: (0, 0, 0, blk_in(s)))
    row = pl.BlockSpec((1, PEER_HEADS, BF16_ROWS, tb), lambda s: (s % nc, 0, 0, blk_in(s)))
    return pl.pallas_call(
        functools.partial(_peer_kernel, tw=tw, n_chunks=nc),
        grid=(nb * nc + 1,),
        in_specs=[pl.BlockSpec((d, tb), lambda s: (0, blk_in(s))), tbl, tbl, row, row,
                  pl.BlockSpec((ec, d), lambda s: (s % nc, 0)),
                  pl.BlockSpec((d, ec), lambda s: (0, jnp.maximum(s - 1, 0) % nc))],
        out_specs=pl.BlockSpec((d, tb), lambda s: (0, blk_out(s))),
        out_shape=jax.ShapeDtypeStruct((d, n), F32),
        scratch_shapes=[pltpu.VMEM((2, ec, tb), BF16)],
        compiler_params=_cparams(("arbitrary",)),
        name="peer_mix",
    )(h2t, rank2, e2, nrow, e1z, u_b, vt_b)


def _final_kernel(ft_ref, x1_ref, mod_ref, g_ref, b_ref, o_ref, *, d_model, alpha, seg):
    d = d_model
    tm = x1_ref.shape[0]
    f = ft_ref[...].T
    for sgm in range(tm // seg):
        rows = slice(sgm * seg, (sgm + 1) * seg)
        gt2 = mod_ref[sgm, :, 5 * d:6 * d]
        o_ref[rows, :] = _layer_norm(alpha * x1_ref[rows, :] + (1.0 + gt2) * f[rows], g_ref[...], b_ref[...])


def _final(ft, x1, mod3, g, b, tm, seg, t_batch, alpha):
    n, d = x1.shape
    nseg = tm // seg
    return pl.pallas_call(
        functools.partial(_final_kernel, d_model=d, alpha=alpha, seg=seg),
        grid=(n // tm,),
        in_specs=[pl.BlockSpec((d, tm), lambda i: (0, i)), pl.BlockSpec((tm, d), lambda i: (i, 0)),
                  pl.BlockSpec((nseg, 1, mod3.shape[2]), lambda i: ((i * tm // t_batch) // nseg, 0, 0)),
                  pl.BlockSpec(g.shape, lambda i: (0, 0)), pl.BlockSpec(b.shape, lambda i: (0, 0))],
        out_specs=pl.BlockSpec((tm, d), lambda i: (i, 0)),
        out_shape=jax.ShapeDtypeStruct((n, d), F32),
        compiler_params=_cparams(("parallel",)),
        name="final_norm",
    )(ft, x1, mod3, g, b)


def _rope_tables(past, t):
    half = QK_ROPE // 2
    inv_freq = jnp.power(jnp.float32(ROPE_BASE), -jnp.arange(half, dtype=F32) * (2.0 / QK_ROPE))
    pos = past + jnp.arange(t, dtype=jnp.int32)
    ang = pos.astype(F32)[:, None] * inv_freq[None, :]
    cos = jnp.cos(ang)
    sin = jnp.sin(ang)
    cos_t = jnp.tile(jnp.concatenate([cos, cos], axis=1), (1, MLA_HEADS))
    sin_t = jnp.tile(jnp.concatenate([-sin, sin], axis=1), (1, MLA_HEADS))
    return cos_t, sin_t


def _swap_halves(w):
    half = QK_ROPE // 2
    return jnp.concatenate([w[..., half:], w[..., :half]], axis=-1)


def _prepare_layer(w_in, g_q, g_kv, w_uq, w_uk, w_uv, g_v, b_v, w_s, b_s, g_out_a, g_out_b, w_o,
                   ln1_g, ln1_b, w_pq, sub_keys, peer_u, peer_v, ln2_g, ln2_b):
    o1 = Q_LORA
    o2 = o1 + KV_LORA
    o3 = o2 + QK_ROPE
    o4 = o3 + GMLP_WIDTH
    kr_w = w_in[:, o2:o3]
    kr_sw = _swap_halves(kr_w)
    w_in_x = jnp.concatenate([w_in[:, :o2], w_in[:, o3:o4], w_in[:, o4:], kr_w, kr_w, kr_sw, kr_sw], axis=1)
    uq = w_uq.reshape(Q_LORA, MLA_HEADS, QK_NOPE + QK_ROPE)
    uq_rope = uq[:, :, QK_NOPE:]
    w_uq_x = jnp.concatenate([uq[:, :, :QK_NOPE].reshape(Q_LORA, -1), uq_rope.reshape(Q_LORA, -1),
                              _swap_halves(uq_rope).reshape(Q_LORA, -1)], axis=1)
    row = lambda a: a.reshape(1, -1)
    groups = []
    for k1, cnt in enumerate(CAND_COUNTS):
        groups += [k1] * cnt
    groups += [-1] * (CAND_ROWS - len(groups))
    cand_groups = (jnp.arange(PEER_TOPK)[:, None] == jnp.asarray(groups)[None, :]).astype(BF16)
    half = D_KEY // 2
    w_pq_t = w_pq.reshape(-1, PEER_HEADS, 2, half).transpose(2, 1, 3, 0).reshape(2 * PEER_HEADS * half, -1)
    sub_keys_bd = jnp.einsum("hpkd,hg->pkhgd", sub_keys, jnp.eye(PEER_HEADS, dtype=sub_keys.dtype))
    sub_keys_bd = sub_keys_bd.reshape(2, N_KEYS * PEER_HEADS, PEER_HEADS * half)
    src = jnp.arange(BF16_ROWS * PEER_HEADS)
    key_perm = ((src % BF16_ROWS) * PEER_HEADS + src // BF16_ROWS)[:, None] == src[None, :]
    return dict(
        sub_keys_bd=sub_keys_bd.astype(BF16), key_perm=key_perm.astype(BF16),
        w_in=w_in_x.astype(BF16), g_q=row(g_q), g_kv=row(g_kv), w_uq=w_uq_x.astype(BF16),
        w_ukv=jnp.concatenate([w_uk, w_uv], axis=1).astype(BF16), g_v=row(g_v), b_v=row(b_v), w_s=w_s,
        b_s=jnp.broadcast_to(b_s[:, :, None], b_s.shape + (GMLP_HEAD_DIM,)), g_out_a=row(g_out_a),
        g_out_b=row(g_out_b), w_o=w_o.astype(BF16), ln1_g=row(ln1_g), ln1_b=row(ln1_b),
        w_pq_t=w_pq_t.astype(BF16), sub_keys=sub_keys.reshape(PEER_HEADS * 2, N_KEYS, D_KEY // 2).astype(BF16),
        cand_groups=cand_groups, peer_u=peer_u.astype(BF16), peer_vt=peer_v.T.astype(BF16),
        ln2_g=row(ln2_g), ln2_b=row(ln2_b),
        kr_dup=jnp.concatenate([jnp.eye(QK_ROPE, dtype=BF16)] * 2, axis=1),
    )


def _pick_tile(n, prefs):
    for p in prefs:
        if n % p == 0:
            return p
    raise ValueError(f"no tile for {n}")


def _layer_group(x, mod, cache_ckv, cache_kr, w, ln_g, ln_b, alpha):
    b, t, d = x.shape
    n = b * t
    past = 0 if cache_ckv is None else cache_ckv.shape[1]
    mod3 = mod.reshape(b, 1, mod.shape[1])
    cos, sin = _rope_tables(past, t)

    tm = _pick_tile(t, (256, 128)) if t >= GMLP_CHUNK else t
    outs = _input_projection(x, ln_g, ln_b, mod3, w, cos, sin, tm, cache_ckv is not None)
    q, k_new, v_new, ckv, kr, bn = outs[:6]
    v_rows = outs[6] if cache_ckv is not None else None

    tq = _pick_tile(t, (512, 256, 128, 64))
    if cache_ckv is None:
        tk = _pick_tile(t, (512, 256))
        k_all, v_all, n_valid = k_new, v_new, t
    else:
        k_c, v_c = _cache_projection(cache_ckv, cache_kr, w["w_ukv"], w["kr_dup"], _pick_tile(past, (512, 256, 128)))
        n_valid = past + t
        pad = (-n_valid) % 256
        tk = _pick_tile(n_valid + pad, (768, 512, 256))
        k_all = jnp.concatenate([k_c, k_new, jnp.zeros((b, pad, k_new.shape[2]), BF16)], axis=1)
        v_all = jnp.concatenate([v_c, v_new, jnp.zeros((b, v_new.shape[1], pad), BF16)], axis=2)
    an = _attention(q, k_all, v_all, w["g_out_a"], past, n_valid, tq, tk)

    tr = _pick_tile(n, (256, 128))
    seg = min(t, tr)
    assert tr % seg == 0 and t % seg == 0
    x1, h2t, rank2, e2, nrow, e1z = _route(x.reshape(n, d), an.reshape(n, -1), bn.reshape(n, -1), ln_g, ln_b,
                                          mod3, w, tr, seg, t, alpha)
    tb = _pick_tile(n, (512, 256))
    ft = _peer_mix(h2t, rank2, e2, nrow, e1z, w["peer_u"], w["peer_vt"], tb, 256, BF16_ROWS * N_KEYS)
    y = _final(ft, x1, mod3, w["ln2_g"], w["ln2_b"], tr, seg, t, alpha)
    return y.reshape(b, t, d), ckv, kr, v_rows


def kernel(x_prompt, x_sample, cache_ckv, cache_krope, c_prompt, c_sample, ln_in_g, ln_in_b, w_ada, b_ada, w_in,
           g_q, g_kv, w_uq, w_uk, w_uv, g_v, b_v, w_s, b_s, g_out_a, g_out_b, w_o, ln1_g, ln1_b, w_pq, sub_keys,
           peer_u, peer_v, ln2_g, ln2_b):
    depth = w_ada.shape[0]
    assert depth == 1, "the entry LayerNorm is fused into the first layer's kernels"
    alpha = (2 * depth) ** 0.25
    bp = c_prompt.shape[0]
    bs = c_sample.shape[0]
    c_all = jnp.concatenate([c_prompt, c_sample], axis=0)
    c_all = jnp.pad(c_all, ((0, (-c_all.shape[0]) % 8), (0, 0)))
    ln_g = ln_in_g.reshape(1, -1)
    ln_b = ln_in_b.reshape(1, -1)
    hp, hs = x_prompt, x_sample
    ckv_p, kr_p, ckv_s, kr_s, gv_s = [], [], [], [], []
    for l in range(depth):
        w = _prepare_layer(w_in[l], g_q[l], g_kv[l], w_uq[l], w_uk[l], w_uv[l], g_v[l], b_v[l], w_s[l], b_s[l],
                           g_out_a[l], g_out_b[l], w_o[l], ln1_g[l], ln1_b[l], w_pq[l], sub_keys[l], peer_u[l],
                           peer_v[l], ln2_g[l], ln2_b[l])
        mod = _ada_mod(c_all, w_ada[l], b_ada[l])
        hp, ckv_new_p, kr_new_p, _ = _layer_group(hp, mod[:bp], None, None, w, ln_g, ln_b, alpha)
        hs, ckv_new_s, kr_new_s, v_new_s = _layer_group(hs, mod[bp:bp + bs], cache_ckv[l], cache_krope[l], w,
                                                       ln_g, ln_b, alpha)
        ckv_p.append(ckv_new_p)
        kr_p.append(kr_new_p)
        ckv_s.append(ckv_new_s)
        kr_s.append(kr_new_s)
        gv_s.append(v_new_s)
    return (hp, hs, jnp.stack(ckv_p), jnp.stack(kr_p), jnp.stack(ckv_s), jnp.stack(kr_s), jnp.stack(gv_s))
```

```python
import functools
import math

import jax
import jax.numpy as jnp
from jax import lax
from jax.experimental import pallas as pl
from jax.experimental.pallas import tpu as pltpu

F32 = jnp.float32
BF16 = jnp.bfloat16

CHUNK = 64
MLA_HEADS = 4
QK_NOPE = 128
QK_ROPE = 64
V_HEAD = 128
Q_LORA = 768
KV_LORA = 256
ROPE_BASE = 10000.0
HEAD_PAD = 2 * QK_NOPE
GMLP_HEADS = 4
GMLP_HEAD_DIM = 128
GMLP_CHUNK = 128
MLA_WIDTH = MLA_HEADS * V_HEAD
GMLP_WIDTH = GMLP_HEADS * GMLP_HEAD_DIM
PEER_HEADS = 8
N_KEYS = 128
PEER_TOPK = 16
D_KEY = 256
EPS = 1e-6
NEG = -1e30
LOWEST = -3.0e38

LANES = 128
SUBLANES = 8
BF16_ROWS = 16
VMEM_LIMIT = 56 * 1024 * 1024

CAND_COUNTS = tuple(PEER_TOPK // (k + 1) for k in range(PEER_TOPK))
CAND_ROWS = 64


def _cparams(sem):
    return pltpu.CompilerParams(dimension_semantics=sem, vmem_limit_bytes=VMEM_LIMIT)


def _gelu(x):
    c1 = -2.0 * math.sqrt(2.0 / math.pi) * math.log2(math.e)
    return x / (1.0 + jnp.exp2(x * (c1 + (c1 * 0.044715) * (x * x))))


def _layer_norm(x, g, b):
    mu = jnp.mean(x, axis=-1, keepdims=True)
    xc = x - mu
    var = jnp.mean(xc * xc, axis=-1, keepdims=True)
    return xc * lax.rsqrt(var + EPS) * g + b


def _rms_norm(x, g):
    ms = jnp.mean(x * x, axis=-1, keepdims=True)
    return x * lax.rsqrt(ms + EPS) * g


def _dot(a, b):
    return jnp.dot(a, b, preferred_element_type=F32)


def _transpose(x):
    return x.T


def _dot_nt(a, b):
    return lax.dot_general(a, b, (((1,), (1,)), ((), ())), preferred_element_type=F32)


def _ada_kernel(c_ref, w_ref, b_ref, o_ref):
    c = c_ref[...]
    s = c * (1.0 / (1.0 + jnp.exp(-c)))
    o_ref[...] = _dot(s.astype(BF16), w_ref[...].astype(BF16)) + b_ref[...]


def _ada_mod(c, w_ada, b_ada):
    bc, d = c.shape
    n = w_ada.shape[1]
    tn = 1536
    return pl.pallas_call(
        _ada_kernel,
        grid=(n // tn,),
        in_specs=[pl.BlockSpec((bc, d), lambda j: (0, 0)),
                  pl.BlockSpec((d, tn), lambda j: (0, j)),
                  pl.BlockSpec((1, tn), lambda j: (0, j))],
        out_specs=pl.BlockSpec((bc, tn), lambda j: (0, j)),
        out_shape=jax.ShapeDtypeStruct((bc, n), F32),
        compiler_params=_cparams(("arbitrary",)),
        name="ada_mod",
    )(c, w_ada, b_ada.reshape(1, n))


def _proj_kernel(x_ref, lng_ref, lnb_ref, mod_ref, win_ref, gq_ref, gkv_ref, wuq_ref, wukv_ref,
                 gv_ref, bv_ref, ws_ref, bs_ref, gob_ref, cos_ref, sin_ref,
                 q_ref, k_ref, vt_ref, ckv_ref, kr_ref, bn_ref, *vrow_refs, d_model):
    d = d_model
    tm = x_ref.shape[0]
    h0 = _layer_norm(x_ref[...], lng_ref[...], lnb_ref[...])
    sh1 = mod_ref[:, 0:d]
    sc1 = mod_ref[:, d:2 * d]
    h = h0 * (1.0 + sc1) + sh1
    z = _dot(h.astype(BF16), win_ref[...])
    o1 = Q_LORA
    o2 = o1 + KV_LORA
    o3 = o2 + GMLP_WIDTH
    o4 = o3 + GMLP_WIDTH
    o5 = o4 + LANES
    cos = cos_ref[...]
    sin = sin_ref[...]

    cq = _rms_norm(z[:, 0:o1], gq_ref[...])
    qa = _dot(cq.astype(BF16), wuq_ref[...])
    nw = MLA_HEADS * QK_NOPE
    rw = MLA_HEADS * QK_ROPE
    scale = math.log2(math.e) / math.sqrt(QK_NOPE + QK_ROPE)
    qr = (qa[:, nw:nw + rw] * cos + qa[:, nw + rw:nw + 2 * rw] * sin) * scale
    lane = lax.broadcasted_iota(jnp.int32, (tm, LANES), 1)
    for hd in range(MLA_HEADS):
        base = hd * HEAD_PAD
        q_ref[:, base:base + QK_NOPE] = (qa[:, hd * QK_NOPE:(hd + 1) * QK_NOPE] * scale).astype(BF16)
        tile = qr[:, (hd // 2) * LANES:(hd // 2 + 1) * LANES]
        keep = (lane < QK_ROPE) if hd % 2 == 0 else (lane >= QK_ROPE)
        q_ref[:, base + QK_NOPE:base + HEAD_PAD] = jnp.where(keep, tile, 0.0).astype(BF16)

    ckv = _rms_norm(z[:, o1:o2], gkv_ref[...])
    ckv_ref[...] = ckv
    kv = _dot(ckv.astype(BF16), wukv_ref[...])
    krd = z[:, o4:o5] * cos[:, 0:LANES] + z[:, o5:o5 + LANES] * sin[:, 0:LANES]
    kr_ref[...] = krd[:, 0:QK_ROPE]
    krd_b = krd.astype(BF16)
    for hd in range(MLA_HEADS):
        base = hd * HEAD_PAD
        k_ref[:, base:base + QK_NOPE] = kv[:, hd * QK_NOPE:(hd + 1) * QK_NOPE].astype(BF16)
        k_ref[:, base + QK_NOPE:base + HEAD_PAD] = krd_b
    vt_ref[...] = _transpose(kv[:, nw:nw + MLA_WIDTH]).astype(BF16)

    u = _gelu(z[:, o2:o3])
    v = _layer_norm(_gelu(z[:, o3:o4]), gv_ref[...], bv_ref[...])
    if vrow_refs:
        vrow_refs[0][...] = v
    vb = v.astype(BF16)
    cr = min(tm, GMLP_CHUNK)
    ri = lax.broadcasted_iota(jnp.int32, (cr, cr), 0) // CHUNK
    ci = lax.broadcasted_iota(jnp.int32, (cr, cr), 1) // CHUNK
    chunks = []
    for c in range(tm // cr):
        heads = []
        for hd in range(GMLP_HEADS):
            w = jnp.where(ri >= ci, ws_ref[hd, 0:cr, 0:cr], 0.0).astype(BF16)
            cols = slice(hd * GMLP_HEAD_DIM, (hd + 1) * GMLP_HEAD_DIM)
            mixed = _dot(w, vb[c * cr:(c + 1) * cr, cols]) + bs_ref[hd, 0:cr, :]
            heads.append(u[c * cr:(c + 1) * cr, cols] * mixed)
        chunks.append(jnp.concatenate(heads, axis=1))
    b_out = chunks[0] if len(chunks) == 1 else jnp.concatenate(chunks, axis=0)
    bn_ref[...] = _rms_norm(b_out, gob_ref[...]).astype(BF16)


def _input_projection(x, ln_g, ln_b, mod3, w, cos, sin, tm, with_vrows):
    b, t, d = x.shape
    assert t % tm == 0 and (tm % GMLP_CHUNK == 0 or (tm == t and t < GMLP_CHUNK))
    full = lambda a: pl.BlockSpec(a.shape, lambda bi, i: (0,) * a.ndim)
    tok = lambda width: pl.BlockSpec((None, tm, width), lambda bi, i: (bi, i, 0))
    weights = [w["w_in"], w["g_q"], w["g_kv"], w["w_uq"], w["w_ukv"], w["g_v"], w["b_v"], w["w_s"], w["b_s"],
               w["g_out_b"]]
    out_widths = [(MLA_HEADS * HEAD_PAD, BF16), (MLA_HEADS * HEAD_PAD, BF16), None,
                  (KV_LORA, F32), (QK_ROPE, F32), (GMLP_WIDTH, BF16)]
    if with_vrows:
        out_widths.append((GMLP_WIDTH, F32))
    out_specs = [pl.BlockSpec((None, MLA_WIDTH, tm), lambda bi, i: (bi, 0, i)) if o is None else tok(o[0])
                 for o in out_widths]
    out_shape = [jax.ShapeDtypeStruct((b, MLA_WIDTH, t), BF16) if o is None
                 else jax.ShapeDtypeStruct((b, t, o[0]), o[1]) for o in out_widths]
    return pl.pallas_call(
        functools.partial(_proj_kernel, d_model=d),
        grid=(b, t // tm),
        in_specs=[tok(d), full(ln_g), full(ln_b),
                  pl.BlockSpec((None, 1, mod3.shape[2]), lambda bi, i: (bi, 0, 0))]
                 + [full(a) for a in weights]
                 + [pl.BlockSpec((tm, cos.shape[1]), lambda bi, i: (i, 0))] * 2,
        out_specs=out_specs,
        out_shape=out_shape,
        compiler_params=_cparams(("parallel", "parallel")),
        name="input_projection",
    )(x, ln_g, ln_b, mod3, *weights, cos, sin)


def _cache_kernel(ckv_ref, kr_ref, wukv_ref, dup_ref, k_ref, vt_ref):
    kv = _dot(ckv_ref[...].astype(BF16), wukv_ref[...])
    krd = _dot(kr_ref[...].astype(BF16), dup_ref[...]).astype(BF16)
    nw = MLA_HEADS * QK_NOPE
    for hd in range(MLA_HEADS):
        base = hd * HEAD_PAD
        k_ref[:, base:base + QK_NOPE] = kv[:, hd * QK_NOPE:(hd + 1) * QK_NOPE].astype(BF16)
        k_ref[:, base + QK_NOPE:base + HEAD_PAD] = krd
    vt_ref[...] = _transpose(kv[:, nw:nw + MLA_WIDTH]).astype(BF16)


def _cache_projection(cache_ckv, cache_kr, w_ukv, dup, tm):
    b, p, _ = cache_ckv.shape
    assert p % tm == 0
    tok = lambda width: pl.BlockSpec((None, tm, width), lambda bi, i: (bi, i, 0))
    full = lambda a: pl.BlockSpec(a.shape, lambda bi, i: (0,) * a.ndim)
    return pl.pallas_call(
        _cache_kernel,
        grid=(b, p // tm),
        in_specs=[tok(KV_LORA), tok(QK_ROPE), full(w_ukv), full(dup)],
        out_specs=[tok(MLA_HEADS * HEAD_PAD), pl.BlockSpec((None, MLA_WIDTH, tm), lambda bi, i: (bi, 0, i))],
        out_shape=[jax.ShapeDtypeStruct((b, p, MLA_HEADS * HEAD_PAD), BF16),
                   jax.ShapeDtypeStruct((b, MLA_WIDTH, p), BF16)],
        compiler_params=_cparams(("parallel", "parallel")),
        name="cache_projection",
    )(cache_ckv, cache_kr, w_ukv, dup)


def _attn_kernel(q_ref, k_ref, vt_ref, ga_ref, o_ref, *, past, n_valid, tq, tk):
    i = pl.program_id(1)
    q_first = past + i * tq
    lo = jnp.minimum((q_first // CHUNK + 1) * CHUNK, n_valid)
    hi = jnp.minimum(((q_first + tq - 1) // CHUNK + 1) * CHUNK, n_valid)
    n_full = lo // tk
    n_tiles = (hi + tk - 1) // tk
    q_chunk = (q_first + lax.broadcasted_iota(jnp.int32, (tk, tq), 1)) // CHUNK
    k_iota = lax.broadcasted_iota(jnp.int32, (tk, tq), 0)

    def make_step(masked):
        def step(j, carry):
            off = pl.multiple_of(j * tk, tk)
            if masked:
                k_pos = off + k_iota
                visible = (k_pos // CHUNK <= q_chunk) & (k_pos < n_valid)
            heads = range(MLA_HEADS)
            scores = [_dot_nt(k_ref[pl.ds(off, tk), hd * HEAD_PAD:(hd + 1) * HEAD_PAD],
                              q_ref[:, hd * HEAD_PAD:(hd + 1) * HEAD_PAD]) for hd in heads]
            probs, stats = [], []
            for hd in heads:
                m, l, _ = carry[hd]
                s = scores[hd]
                if masked:
                    s = jnp.where(visible, s, NEG)
                m_new = jnp.maximum(m, jnp.max(s, axis=0, keepdims=True))
                alpha = jnp.exp2(m - m_new)
                p = jnp.exp2(s - m_new)
                stats.append((m_new, alpha * l + jnp.sum(p, axis=0, keepdims=True), alpha))
                probs.append(p.astype(BF16))
            new = []
            for hd in heads:
                m_new, l, alpha = stats[hd]
                vth = vt_ref[hd * V_HEAD:(hd + 1) * V_HEAD, pl.ds(off, tk)]
                new.append((m_new, l, alpha * carry[hd][2] + _dot(vth, probs[hd])))
            return tuple(new)
        return step

    init = tuple((jnp.full((1, tq), NEG, F32), jnp.zeros((1, tq), F32), jnp.zeros((V_HEAD, tq), F32))
                 for _ in range(MLA_HEADS))
    carry = lax.fori_loop(0, n_full, make_step(False), init)
    carry = lax.fori_loop(n_full, n_tiles, make_step(True), carry)
    a_t = jnp.concatenate([acc / l for _, l, acc in carry], axis=0)
    o_ref[...] = _rms_norm(_transpose(a_t), ga_ref[...]).astype(BF16)


def _attention(q, k, vt, g_out_a, past, n_valid, tq, tk):
    b, t, _ = q.shape
    lk = k.shape[1]
    assert t % tq == 0 and lk % tk == 0 and n_valid <= lk and vt.shape[2] == lk
    return pl.pallas_call(
        functools.partial(_attn_kernel, past=past, n_valid=n_valid, tq=tq, tk=tk),
        grid=(b, t // tq),
        in_specs=[pl.BlockSpec((None, tq, q.shape[2]), lambda bi, i: (bi, i, 0)),
                  pl.BlockSpec((None, lk, k.shape[2]), lambda bi, i: (bi, 0, 0)),
                  pl.BlockSpec((None, vt.shape[1], lk), lambda bi, i: (bi, 0, 0)),
                  pl.BlockSpec(g_out_a.shape, lambda bi, i: (0, 0))],
        out_specs=pl.BlockSpec((None, tq, MLA_WIDTH), lambda bi, i: (bi, i, 0)),
        out_shape=jax.ShapeDtypeStruct((b, t, MLA_WIDTH), BF16),
        compiler_params=_cparams(("parallel", "arbitrary")),
        name="attention",
    )(q, k, vt, g_out_a)


def _extract_top(s, rounds):
    r_, l_ = s.shape
    ridx = lax.broadcasted_iota(jnp.int32, (r_, l_), 0).astype(F32)
    vidx = lax.broadcasted_iota(jnp.int32, (rounds, l_), 0)
    rank = jnp.full((r_, l_), float(rounds), F32)
    vals = jnp.zeros((rounds, l_), F32)
    for r in range(rounds):
        m = jnp.max(s, axis=0, keepdims=True)
        first = jnp.min(jnp.where(s == m, ridx, float(r_)), axis=0, keepdims=True)
        hit = ridx == first
        rank = jnp.where(hit, float(r), rank)
        s = jnp.where(hit, LOWEST, s)
        vals = jnp.where(vidx == r, m, vals)
    return rank, vals


def _route_kernel(x_ref, an_ref, bn_ref, lng_ref, lnb_ref, mod_ref, wo_ref, l1g_ref, l1b_ref, ws_ref,
                  perm_ref, grp_ref, x1_ref, h2t_ref, rank2_ref, e2_ref, nrow_ref, e1z_ref, sc_ref,
                  cand_ref, *, d_model, alpha, seg):
    d = d_model
    tm = x_ref.shape[0]
    h0 = _layer_norm(x_ref[...], lng_ref[...], lnb_ref[...])
    mix = _dot(an_ref[...], wo_ref[0:MLA_WIDTH, :]) + _dot(bn_ref[...], wo_ref[MLA_WIDTH:, :])
    parts = []
    for sgm in range(tm // seg):
        rows = slice(sgm * seg, (sgm + 1) * seg)
        gt1 = mod_ref[sgm, :, 2 * d:3 * d]
        sh2 = mod_ref[sgm, :, 3 * d:4 * d]
        sc2 = mod_ref[sgm, :, 4 * d:5 * d]
        x1 = _layer_norm(alpha * h0[rows] + (1.0 + gt1) * mix[rows], l1g_ref[...], l1b_ref[...])
        x1_ref[rows, :] = x1
        parts.append(x1 * (1.0 + sc2) + sh2)
    h2 = parts[0] if len(parts) == 1 else jnp.concatenate(parts, axis=0)
    h2t = h2.T.astype(BF16)
    h2t_ref[...] = h2t
    rows = PEER_HEADS * N_KEYS
    x1a = _dot(ws_ref[0], h2t)
    x2a = _dot(ws_ref[1], h2t)
    for g in range(tm // LANES):
        sc_ref[0, g] = x1a[:, g * LANES:(g + 1) * LANES]
        sc_ref[1, g] = x2a[:, g * LANES:(g + 1) * LANES]
    x1 = [x1a[k * PEER_HEADS:(k + 1) * PEER_HEADS, :] for k in range(N_KEYS)]
    x2 = [x2a[k * PEER_HEADS:(k + 1) * PEER_HEADS, :] for k in range(N_KEYS)]
    a, gap1 = _top_sorted(x1)
    b, gap2 = _top_sorted(x2)
    cand = [a[k1] + b[k2] for k1, cnt in enumerate(CAND_COUNTS) for k2 in range(cnt)]
    lowest = jnp.full((PEER_HEADS, tm), LOWEST, F32)
    top, gap3 = _top_sorted(cand + [lowest] * (CAND_ROWS - len(cand)))
    gap = jnp.minimum(jnp.minimum(gap1, gap2), gap3)
    thr = top[PEER_TOPK - 1]
    z = None
    n = []
    off = 0
    for k1, cnt in enumerate(CAND_COUNTS):
        nk = None
        for c in cand[off:off + cnt]:
            sel = c >= thr
            one = jnp.where(sel, 1.0, 0.0)
            nk = one if nk is None else nk + one
            e = jnp.where(sel, jnp.exp(c - cand[0]), 0.0)
            z = e if z is None else z + e
        n.append(nk)
        off += cnt
    zinv = 1.0 / z
    x1k = x1a.reshape(N_KEYS, PEER_HEADS, tm)
    x2k = x2a.reshape(N_KEYS, PEER_HEADS, tm)
    nrow = jnp.zeros_like(x1k)
    rank2 = jnp.full(x2k.shape, float(PEER_TOPK), F32)
    for r in range(PEER_TOPK):
        nrow = jnp.where(x1k == a[r][None], n[r][None], nrow)
        rank2 = jnp.where(x2k == b[r][None], float(r), rank2)

    blk = BF16_ROWS * PEER_HEADS
    for out_ref, v in ((rank2_ref, rank2), (e2_ref, jnp.exp(x2k - b[0][None])), (nrow_ref, nrow),
                       (e1z_ref, jnp.exp(x1k - a[0][None]) * zinv[None])):
        tab = v.reshape(rows, tm).astype(BF16)
        for kb in range(rows // blk):
            out_ref[kb * blk:(kb + 1) * blk, :] = _dot(perm_ref[...], tab[kb * blk:(kb + 1) * blk, :]).astype(BF16)

    pad0 = sum(CAND_COUNTS) // SUBLANES * SUBLANES
    cand_ref[pad0:CAND_ROWS, :] = jnp.full((CAND_ROWS - pad0, tm), LOWEST, F32)
    for hd in range(PEER_HEADS):
        @pl.when(jnp.min(gap[hd:hd + 1, :]) <= 0.0)
        def _(hd=hd):
            _route_head_by_extraction(hd, sc_ref, grp_ref, rank2_ref, e2_ref, nrow_ref, e1z_ref, cand_ref)


def _top_sorted(vals):
    x = list(vals)
    n = len(x)
    k = PEER_TOPK

    def order(i, j):
        hi = jnp.maximum(x[i], x[j])
        x[j] = jnp.minimum(x[i], x[j])
        x[i] = hi

    for base in range(0, n, k):
        size = 2
        while size <= k:
            j = size // 2
            while j >= 1:
                for i in range(k):
                    l = i ^ j
                    if l > i:
                        if i & size == 0:
                            order(base + i, base + l)
                        else:
                            order(base + l, base + i)
                j //= 2
            size *= 2
    dropped = None
    step = k
    while step < n:
        for base in range(0, n, 2 * step):
            for i in range(k):
                lo = jnp.minimum(x[base + i], x[base + step + k - 1 - i])
                x[base + i] = jnp.maximum(x[base + i], x[base + step + k - 1 - i])
                dropped = lo if dropped is None else jnp.maximum(dropped, lo)
            j = k // 2
            while j >= 1:
                for i in range(k):
                    l = i ^ j
                    if l > i:
                        order(base + i, base + l)
                j //= 2
        step *= 2
    top = x[:k]
    gap = top[k - 1] - dropped
    for r in range(k - 1):
        gap = jnp.minimum(gap, top[r] - top[r + 1])
    return top, gap


def _route_head_by_extraction(hd, sc_ref, grp_ref, rank2_ref, e2_ref, nrow_ref, e1z_ref, cand_ref):
    lane_tiles = sc_ref.shape[1]
    tm = lane_tiles * LANES
    head_rows = pl.ds(hd, N_KEYS, stride=PEER_HEADS)
    s1 = jnp.concatenate([sc_ref[0, g, head_rows, :] for g in range(lane_tiles)], axis=1)
    s2 = jnp.concatenate([sc_ref[1, g, head_rows, :] for g in range(lane_tiles)], axis=1)
    rank1, a = _extract_top(s1, PEER_TOPK)
    rank2, bvals = _extract_top(s2, PEER_TOPK)
    off = 0
    for k1, cnt in enumerate(CAND_COUNTS):
        cand_ref[off:off + cnt, :] = a[k1:k1 + 1, :] + bvals[0:cnt, :]
        off += cnt
    cand = cand_ref[...]
    crank, _ = _extract_top(cand, PEER_TOPK)
    sel = crank < float(PEER_TOPK)
    z = jnp.sum(jnp.where(sel, jnp.exp(cand - cand[0:1, :]), 0.0), axis=0, keepdims=True)
    n = _dot(grp_ref[...], jnp.where(sel, 1.0, 0.0).astype(BF16))
    nrow = jnp.zeros((N_KEYS, tm), F32)
    for k1 in range(PEER_TOPK):
        nrow = jnp.where(rank1 == float(k1), n[k1:k1 + 1, :], nrow)
    blk = BF16_ROWS * PEER_HEADS
    for out_ref, v in ((rank2_ref, rank2), (e2_ref, jnp.exp(s2 - bvals[0:1, :])), (nrow_ref, nrow),
                       (e1z_ref, jnp.exp(s1 - a[0:1, :]) / z)):
        vb = v.astype(BF16)
        for kb in range(N_KEYS // BF16_ROWS):
            out_ref[kb * blk + hd * BF16_ROWS:kb * blk + (hd + 1) * BF16_ROWS, :] = (
                vb[kb * BF16_ROWS:(kb + 1) * BF16_ROWS, :])


def _route(x2d, an2d, bn2d, ln_g, ln_b, mod3, w, tm, seg, t_batch, alpha):
    n, d = x2d.shape
    assert n % tm == 0 and tm % seg == 0 and tm % LANES == 0 and PEER_HEADS == SUBLANES
    nseg = tm // seg
    rows = PEER_HEADS * N_KEYS
    tok = lambda width: pl.BlockSpec((tm, width), lambda i: (i, 0))
    full = lambda a: pl.BlockSpec(a.shape, lambda i: (0,) * a.ndim)
    tbl = pl.BlockSpec((rows, tm), lambda i: (0, i))
    weights = [w["w_o"], w["ln1_g"], w["ln1_b"], w["w_score"], w["key_perm"], w["cand_groups"]]
    return pl.pallas_call(
        functools.partial(_route_kernel, d_model=d, alpha=alpha, seg=seg),
        grid=(n // tm,),
        in_specs=[tok(d), tok(MLA_WIDTH), tok(GMLP_WIDTH), full(ln_g), full(ln_b),
                  pl.BlockSpec((nseg, 1, mod3.shape[2]), lambda i: ((i * tm // t_batch) // nseg, 0, 0))]
                 + [full(a) for a in weights],
        out_specs=[tok(d), pl.BlockSpec((d, tm), lambda i: (0, i)), tbl, tbl, tbl, tbl],
        out_shape=[jax.ShapeDtypeStruct((n, d), F32), jax.ShapeDtypeStruct((d, n), BF16)]
                  + [jax.ShapeDtypeStruct((rows, n), BF16)] * 4,
        scratch_shapes=[pltpu.VMEM((2, tm // LANES, rows, LANES), F32), pltpu.VMEM((CAND_ROWS, tm), F32)],
        compiler_params=_cparams(("parallel",)),
        name="route",
    )(x2d, an2d, bn2d, ln_g, ln_b, mod3, *weights)


def _peer_kernel(h2t_ref, rank2_ref, e2_ref, nrow_ref, e1z_ref, u_ref, vt_ref, o_ref, a_ref, *, tw, n_chunks):
    s = pl.program_id(0)
    tb = h2t_ref.shape[1]
    rows_per_chunk = u_ref.shape[0] // N_KEYS
    cur = s % 2
    prev = 1 - cur

    @pl.when(s == 0)
    def _():
        a_ref[1] = jnp.zeros(a_ref.shape[1:], BF16)

    @pl.when(jnp.maximum(s - 1, 0) % n_chunks == 0)
    def _():
        o_ref[...] = jnp.zeros_like(o_ref)

    def subtile(ts, carry):
        off = pl.multiple_of(ts * tw, tw)
        o_ref[:, pl.ds(off, tw)] += _dot(vt_ref[...], a_ref[prev, :, pl.ds(off, tw)])
        ht = h2t_ref[:, pl.ds(off, tw)]
        nr_rows = [nrow_ref[0, hd, :, pl.ds(off, tw)].astype(F32) for hd in range(PEER_HEADS)]
        e1_rows = [e1z_ref[0, hd, :, pl.ds(off, tw)].astype(F32) for hd in range(PEER_HEADS)]
        for p in range(rows_per_chunk):
            s = _dot(u_ref[p * N_KEYS:(p + 1) * N_KEYS, :], ht)
            wgt = jnp.zeros((N_KEYS // BF16_ROWS, BF16_ROWS, tw), BF16)
            for hd in range(PEER_HEADS):
                nr = jnp.broadcast_to(nr_rows[hd][p:p + 1, :], (BF16_ROWS, tw)).astype(BF16)
                e1 = jnp.broadcast_to(e1_rows[hd][p:p + 1, :], (BF16_ROWS, tw)).astype(BF16)
                r2 = rank2_ref[:, hd, :, pl.ds(off, tw)]
                e2 = e2_ref[:, hd, :, pl.ds(off, tw)]
                wgt = wgt + jnp.where(r2 < nr[None], e2, jnp.zeros_like(e2)) * e1[None]
            a_ref[cur, p * N_KEYS:(p + 1) * N_KEYS, pl.ds(off, tw)] = (
                _gelu(s).astype(BF16) * wgt.reshape(N_KEYS, tw))
        return carry

    lax.fori_loop(0, tb // tw, subtile, 0)


def _peer_mix(h2t, rank2, e2, nrow, e1z, u_b, vt_b, tb, tw, ec):
    d, n = h2t.shape
    ne = u_b.shape[0]
    assert n % tb == 0 and tb % tw == 0 and ne % ec == 0 and ec == BF16_ROWS * N_KEYS
    groups = N_KEYS // BF16_ROWS
    shape4 = (groups, PEER_HEADS, BF16_ROWS, n)
    rank2, e2, nrow, e1z = (a.reshape(shape4) for a in (rank2, e2, nrow, e1z))
    nc = ne // ec
    nb = n // tb
    blk_in = lambda s: jnp.minimum(s // nc, nb - 1)
    blk_out = lambda s: jnp.maximum(s - 1, 0) // nc
    tbl = pl.BlockSpec((groups, PEER_HEADS, BF16_ROWS, tb), lambda s: (0, 0, 0, blk_in(s)))
    row = pl.BlockSpec((1, PEER_HEADS, BF16_ROWS, tb), lambda s: (s % nc, 0, 0, blk_in(s)))
    return pl.pallas_call(
        functools.partial(_peer_kernel, tw=tw, n_chunks=nc),
        grid=(nb * nc + 1,),
        in_specs=[pl.BlockSpec((d, tb), lambda s: (0, blk_in(s))), tbl, tbl, row, row,
                  pl.BlockSpec((ec, d), lambda s: (s % nc, 0)),
                  pl.BlockSpec((d, ec), lambda s: (0, jnp.maximum(s - 1, 0) % nc))],
        out_specs=pl.BlockSpec((d, tb), lambda s: (0, blk_out(s))),
        out_shape=jax.ShapeDtypeStruct((d, n), F32),
        scratch_shapes=[pltpu.VMEM((2, ec, tb), BF16)],
        compiler_params=_cparams(("arbitrary",)),
        name="peer_mix",
    )(h2t, rank2, e2, nrow, e1z, u_b, vt_b)


def _final_kernel(ft_ref, x1_ref, mod_ref, g_ref, b_ref, o_ref, *, d_model, alpha, seg):
    d = d_model
    tm = x1_ref.shape[0]
    f = ft_ref[...].T
    for sgm in range(tm // seg):
        rows = slice(sgm * seg, (sgm + 1) * seg)
        gt2 = mod_ref[sgm, :, 5 * d:6 * d]
        o_ref[rows, :] = _layer_norm(alpha * x1_ref[rows, :] + (1.0 + gt2) * f[rows], g_ref[...], b_ref[...])


def _final(ft, x1, mod3, g, b, tm, seg, t_batch, alpha):
    n, d = x1.shape
    nseg = tm // seg
    return pl.pallas_call(
        functools.partial(_final_kernel, d_model=d, alpha=alpha, seg=seg),
        grid=(n // tm,),
        in_specs=[pl.BlockSpec((d, tm), lambda i: (0, i)), pl.BlockSpec((tm, d), lambda i: (i, 0)),
                  pl.BlockSpec((nseg, 1, mod3.shape[2]), lambda i: ((i * tm // t_batch) // nseg, 0, 0)),
                  pl.BlockSpec(g.shape, lambda i: (0, 0)), pl.BlockSpec(b.shape, lambda i: (0, 0))],
        out_specs=pl.BlockSpec((tm, d), lambda i: (i, 0)),
        out_shape=jax.ShapeDtypeStruct((n, d), F32),
        compiler_params=_cparams(("parallel",)),
        name="final_norm",
    )(ft, x1, mod3, g, b)


def _combine_kernel(sk_ref, wq_ref, o_ref):
    o_ref[...] = _dot(sk_ref[...], wq_ref[...]).astype(BF16)


def _combine_score_weights(sub_keys_bd, w_pq_t):
    sides, rows, inner = sub_keys_bd.shape
    d = w_pq_t.shape[1]
    return pl.pallas_call(
        _combine_kernel,
        grid=(sides,),
        in_specs=[pl.BlockSpec((None, rows, inner), lambda s: (s, 0, 0)),
                  pl.BlockSpec((inner, d), lambda s: (s, 0))],
        out_specs=pl.BlockSpec((None, rows, d), lambda s: (s, 0, 0)),
        out_shape=jax.ShapeDtypeStruct((sides, rows, d), BF16),
        compiler_params=_cparams(("parallel",)),
        name="score_weights",
    )(sub_keys_bd, w_pq_t)


def _rope_tables(past, t):
    half = QK_ROPE // 2
    inv_freq = jnp.power(jnp.float32(ROPE_BASE), -jnp.arange(half, dtype=F32) * (2.0 / QK_ROPE))
    pos = past + jnp.arange(t, dtype=jnp.int32)
    ang = pos.astype(F32)[:, None] * inv_freq[None, :]
    cos = jnp.cos(ang)
    sin = jnp.sin(ang)
    cos_t = jnp.tile(jnp.concatenate([cos, cos], axis=1), (1, MLA_HEADS))
    sin_t = jnp.tile(jnp.concatenate([-sin, sin], axis=1), (1, MLA_HEADS))
    return cos_t, sin_t


def _swap_halves(w):
    half = QK_ROPE // 2
    return jnp.concatenate([w[..., half:], w[..., :half]], axis=-1)


def _prepare_layer(w_in, g_q, g_kv, w_uq, w_uk, w_uv, g_v, b_v, w_s, b_s, g_out_a, g_out_b, w_o,
                   ln1_g, ln1_b, w_pq, sub_keys, peer_u, peer_v, ln2_g, ln2_b):
    o1 = Q_LORA
    o2 = o1 + KV_LORA
    o3 = o2 + QK_ROPE
    o4 = o3 + GMLP_WIDTH
    kr_w = w_in[:, o2:o3]
    kr_sw = _swap_halves(kr_w)
    w_in_x = jnp.concatenate([w_in[:, :o2], w_in[:, o3:o4], w_in[:, o4:], kr_w, kr_w, kr_sw, kr_sw], axis=1)
    uq = w_uq.reshape(Q_LORA, MLA_HEADS, QK_NOPE + QK_ROPE)
    uq_rope = uq[:, :, QK_NOPE:]
    w_uq_x = jnp.concatenate([uq[:, :, :QK_NOPE].reshape(Q_LORA, -1), uq_rope.reshape(Q_LORA, -1),
                              _swap_halves(uq_rope).reshape(Q_LORA, -1)], axis=1)
    row = lambda a: a.reshape(1, -1)
    groups = []
    for k1, cnt in enumerate(CAND_COUNTS):
        groups += [k1] * cnt
    groups += [-1] * (CAND_ROWS - len(groups))
    cand_groups = (jnp.arange(PEER_TOPK)[:, None] == jnp.asarray(groups)[None, :]).astype(BF16)
    half = D_KEY // 2
    w_pq_t = w_pq.reshape(-1, PEER_HEADS, 2, half).transpose(2, 1, 3, 0).reshape(2 * PEER_HEADS * half, -1)
    sub_keys_bd = jnp.einsum("hpkd,hg->pkhgd", sub_keys, jnp.eye(PEER_HEADS, dtype=sub_keys.dtype))
    sub_keys_bd = sub_keys_bd.reshape(2, N_KEYS * PEER_HEADS, PEER_HEADS * half)
    src = jnp.arange(BF16_ROWS * PEER_HEADS)
    key_perm = ((src % BF16_ROWS) * PEER_HEADS + src // BF16_ROWS)[:, None] == src[None, :]
    return dict(
        w_score=_combine_score_weights(sub_keys_bd.astype(BF16), w_pq_t.astype(BF16)), key_perm=key_perm.astype(BF16),
        w_in=w_in_x.astype(BF16), g_q=row(g_q), g_kv=row(g_kv), w_uq=w_uq_x.astype(BF16),
        w_ukv=jnp.concatenate([w_uk, w_uv], axis=1).astype(BF16), g_v=row(g_v), b_v=row(b_v), w_s=w_s,
        b_s=jnp.broadcast_to(b_s[:, :, None], b_s.shape + (GMLP_HEAD_DIM,)), g_out_a=row(g_out_a),
        g_out_b=row(g_out_b), w_o=w_o.astype(BF16), ln1_g=row(ln1_g), ln1_b=row(ln1_b),
        cand_groups=cand_groups, peer_u=peer_u.astype(BF16), peer_vt=peer_v.T.astype(BF16),
        ln2_g=row(ln2_g), ln2_b=row(ln2_b),
        kr_dup=jnp.concatenate([jnp.eye(QK_ROPE, dtype=BF16)] * 2, axis=1),
    )


def _pick_tile(n, prefs):
    for p in prefs:
        if n % p == 0:
            return p
    raise ValueError(f"no tile for {n}")


def _layer_group(x, mod, cache_ckv, cache_kr, w, ln_g, ln_b, alpha):
    b, t, d = x.shape
    n = b * t
    past = 0 if cache_ckv is None else cache_ckv.shape[1]
    mod3 = mod.reshape(b, 1, mod.shape[1])
    cos, sin = _rope_tables(past, t)

    tm = _pick_tile(t, (256, 128)) if t >= GMLP_CHUNK else t
    outs = _input_projection(x, ln_g, ln_b, mod3, w, cos, sin, tm, cache_ckv is not None)
    q, k_new, v_new, ckv, kr, bn = outs[:6]
    v_rows = outs[6] if cache_ckv is not None else None

    tq = _pick_tile(t, (512, 256, 128, 64))
    if cache_ckv is None:
        tk = _pick_tile(t, (512, 256))
        k_all, v_all, n_valid = k_new, v_new, t
    else:
        k_c, v_c = _cache_projection(cache_ckv, cache_kr, w["w_ukv"], w["kr_dup"], _pick_tile(past, (512, 256, 128)))
        n_valid = past + t
        pad = (-n_valid) % 256
        tk = _pick_tile(n_valid + pad, (768, 512, 256))
        k_all = jnp.concatenate([k_c, k_new, jnp.zeros((b, pad, k_new.shape[2]), BF16)], axis=1)
        v_all = jnp.concatenate([v_c, v_new, jnp.zeros((b, v_new.shape[1], pad), BF16)], axis=2)
    an = _attention(q, k_all, v_all, w["g_out_a"], past, n_valid, tq, tk)

    tr = _pick_tile(n, (256, 128))
    seg = min(t, tr)
    assert tr % seg == 0 and t % seg == 0
    x1, h2t, rank2, e2, nrow, e1z = _route(x.reshape(n, d), an.reshape(n, -1), bn.reshape(n, -1), ln_g, ln_b,
                                          mod3, w, tr, seg, t, alpha)
    tb = _pick_tile(n, (512, 256))
    ft = _peer_mix(h2t, rank2, e2, nrow, e1z, w["peer_u"], w["peer_vt"], tb, 256, BF16_ROWS * N_KEYS)
    y = _final(ft, x1, mod3, w["ln2_g"], w["ln2_b"], tr, seg, t, alpha)
    return y.reshape(b, t, d), ckv, kr, v_rows


def kernel(x_prompt, x_sample, cache_ckv, cache_krope, c_prompt, c_sample, ln_in_g, ln_in_b, w_ada, b_ada, w_in,
           g_q, g_kv, w_uq, w_uk, w_uv, g_v, b_v, w_s, b_s, g_out_a, g_out_b, w_o, ln1_g, ln1_b, w_pq, sub_keys,
           peer_u, peer_v, ln2_g, ln2_b):
    depth = w_ada.shape[0]
    assert depth == 1, "the entry LayerNorm is fused into the first layer's kernels"
    alpha = (2 * depth) ** 0.25
    bp = c_prompt.shape[0]
    bs = c_sample.shape[0]
    c_all = jnp.concatenate([c_prompt, c_sample], axis=0)
    c_all = jnp.pad(c_all, ((0, (-c_all.shape[0]) % 8), (0, 0)))
    ln_g = ln_in_g.reshape(1, -1)
    ln_b = ln_in_b.reshape(1, -1)
    hp, hs = x_prompt, x_sample
    ckv_p, kr_p, ckv_s, kr_s, gv_s = [], [], [], [], []
    for l in range(depth):
        w = _prepare_layer(w_in[l], g_q[l], g_kv[l], w_uq[l], w_uk[l], w_uv[l], g_v[l], b_v[l], w_s[l], b_s[l],
                           g_out_a[l], g_out_b[l], w_o[l], ln1_g[l], ln1_b[l], w_pq[l], sub_keys[l], peer_u[l],
                           peer_v[l], ln2_g[l], ln2_b[l])
        mod = _ada_mod(c_all, w_ada[l], b_ada[l])
        hp, ckv_new_p, kr_new_p, _ = _layer_group(hp, mod[:bp], None, None, w, ln_g, ln_b, alpha)
        hs, ckv_new_s, kr_new_s, v_new_s = _layer_group(hs, mod[bp:bp + bs], cache_ckv[l], cache_krope[l], w,
                                                       ln_g, ln_b, alpha)
        ckv_p.append(ckv_new_p)
        kr_p.append(kr_new_p)
        ckv_s.append(ckv_new_s)
        kr_s.append(kr_new_s)
        gv_s.append(v_new_s)
    return (hp, hs, jnp.stack(ckv_p), jnp.stack(kr_p), jnp.stack(ckv_s), jnp.stack(kr_s), jnp.stack(gv_s))
```

```python
import functools
import math

import jax
import jax.numpy as jnp
from jax import lax
from jax.experimental import pallas as pl
from jax.experimental.pallas import tpu as pltpu

F32 = jnp.float32
BF16 = jnp.bfloat16

CHUNK = 64
MLA_HEADS = 4
QK_NOPE = 128
QK_ROPE = 64
V_HEAD = 128
Q_LORA = 768
KV_LORA = 256
ROPE_BASE = 10000.0
HEAD_PAD = 2 * QK_NOPE
GMLP_HEADS = 4
GMLP_HEAD_DIM = 128
GMLP_CHUNK = 128
MLA_WIDTH = MLA_HEADS * V_HEAD
GMLP_WIDTH = GMLP_HEADS * GMLP_HEAD_DIM
PEER_HEADS = 8
N_KEYS = 128
PEER_TOPK = 16
D_KEY = 256
EPS = 1e-6
NEG = -1e30
LOWEST = -3.0e38

LANES = 128
SUBLANES = 8
BF16_ROWS = 16
VMEM_LIMIT = 56 * 1024 * 1024

CAND_COUNTS = tuple(PEER_TOPK // (k + 1) for k in range(PEER_TOPK))
CAND_ROWS = 64


def _cparams(sem):
    return pltpu.CompilerParams(dimension_semantics=sem, vmem_limit_bytes=VMEM_LIMIT)


def _gelu(x):
    c1 = -2.0 * math.sqrt(2.0 / math.pi) * math.log2(math.e)
    return x / (1.0 + jnp.exp2(x * (c1 + (c1 * 0.044715) * (x * x))))


def _layer_norm(x, g, b):
    mu = jnp.mean(x, axis=-1, keepdims=True)
    xc = x - mu
    var = jnp.mean(xc * xc, axis=-1, keepdims=True)
    return xc * lax.rsqrt(var + EPS) * g + b


def _rms_norm(x, g):
    ms = jnp.mean(x * x, axis=-1, keepdims=True)
    return x * lax.rsqrt(ms + EPS) * g


def _dot(a, b):
    return jnp.dot(a, b, preferred_element_type=F32)


def _transpose(x):
    return x.T


def _dot_nt(a, b):
    return lax.dot_general(a, b, (((1,), (1,)), ((), ())), preferred_element_type=F32)


def _ada_kernel(c_ref, w_ref, b_ref, o_ref):
    c = c_ref[...]
    s = c * (1.0 / (1.0 + jnp.exp(-c)))
    o_ref[...] = _dot(s.astype(BF16), w_ref[...].astype(BF16)) + b_ref[...]


def _ada_mod(c, w_ada, b_ada):
    bc, d = c.shape
    n = w_ada.shape[1]
    tn = 1536
    return pl.pallas_call(
        _ada_kernel,
        grid=(n // tn,),
        in_specs=[pl.BlockSpec((bc, d), lambda j: (0, 0)),
                  pl.BlockSpec((d, tn), lambda j: (0, j)),
                  pl.BlockSpec((1, tn), lambda j: (0, j))],
        out_specs=pl.BlockSpec((bc, tn), lambda j: (0, j)),
        out_shape=jax.ShapeDtypeStruct((bc, n), F32),
        compiler_params=_cparams(("arbitrary",)),
        name="ada_mod",
    )(c, w_ada, b_ada.reshape(1, n))


def _proj_kernel(x_ref, lng_ref, lnb_ref, mod_ref, win_ref, gq_ref, gkv_ref, wuq_ref, wukv_ref,
                 gv_ref, bv_ref, ws_ref, bs_ref, gob_ref, cos_ref, sin_ref,
                 q_ref, k_ref, vt_ref, ckv_ref, kr_ref, bn_ref, *vrow_refs, d_model):
    d = d_model
    tm = x_ref.shape[0]
    h0 = _layer_norm(x_ref[...], lng_ref[...], lnb_ref[...])
    sh1 = mod_ref[:, 0:d]
    sc1 = mod_ref[:, d:2 * d]
    h = h0 * (1.0 + sc1) + sh1
    z = _dot(h.astype(BF16), win_ref[...])
    o1 = Q_LORA
    o2 = o1 + KV_LORA
    o3 = o2 + GMLP_WIDTH
    o4 = o3 + GMLP_WIDTH
    o5 = o4 + LANES
    cos = cos_ref[...]
    sin = sin_ref[...]

    cq = _rms_norm(z[:, 0:o1], gq_ref[...])
    qa = _dot(cq.astype(BF16), wuq_ref[...])
    nw = MLA_HEADS * QK_NOPE
    rw = MLA_HEADS * QK_ROPE
    scale = math.log2(math.e) / math.sqrt(QK_NOPE + QK_ROPE)
    qr = (qa[:, nw:nw + rw] * cos + qa[:, nw + rw:nw + 2 * rw] * sin) * scale
    lane = lax.broadcasted_iota(jnp.int32, (tm, LANES), 1)
    for hd in range(MLA_HEADS):
        base = hd * HEAD_PAD
        q_ref[:, base:base + QK_NOPE] = (qa[:, hd * QK_NOPE:(hd + 1) * QK_NOPE] * scale).astype(BF16)
        tile = qr[:, (hd // 2) * LANES:(hd // 2 + 1) * LANES]
        keep = (lane < QK_ROPE) if hd % 2 == 0 else (lane >= QK_ROPE)
        q_ref[:, base + QK_NOPE:base + HEAD_PAD] = jnp.where(keep, tile, 0.0).astype(BF16)

    ckv = _rms_norm(z[:, o1:o2], gkv_ref[...])
    ckv_ref[...] = ckv
    kv = _dot(ckv.astype(BF16), wukv_ref[...])
    krd = z[:, o4:o5] * cos[:, 0:LANES] + z[:, o5:o5 + LANES] * sin[:, 0:LANES]
    kr_ref[...] = krd[:, 0:QK_ROPE]
    krd_b = krd.astype(BF16)
    for hd in range(MLA_HEADS):
        base = hd * HEAD_PAD
        k_ref[:, base:base + QK_NOPE] = kv[:, hd * QK_NOPE:(hd + 1) * QK_NOPE].astype(BF16)
        k_ref[:, base + QK_NOPE:base + HEAD_PAD] = krd_b
    vt_ref[...] = _transpose(kv[:, nw:nw + MLA_WIDTH]).astype(BF16)

    u = _gelu(z[:, o2:o3])
    v = _layer_norm(_gelu(z[:, o3:o4]), gv_ref[...], bv_ref[...])
    if vrow_refs:
        vrow_refs[0][...] = v
    vb = v.astype(BF16)
    cr = min(tm, GMLP_CHUNK)
    ri = lax.broadcasted_iota(jnp.int32, (cr, cr), 0) // CHUNK
    ci = lax.broadcasted_iota(jnp.int32, (cr, cr), 1) // CHUNK
    chunks = []
    for c in range(tm // cr):
        heads = []
        for hd in range(GMLP_HEADS):
            w = jnp.where(ri >= ci, ws_ref[hd, 0:cr, 0:cr], 0.0).astype(BF16)
            cols = slice(hd * GMLP_HEAD_DIM, (hd + 1) * GMLP_HEAD_DIM)
            mixed = _dot(w, vb[c * cr:(c + 1) * cr, cols]) + bs_ref[hd, 0:cr, :]
            heads.append(u[c * cr:(c + 1) * cr, cols] * mixed)
        chunks.append(jnp.concatenate(heads, axis=1))
    b_out = chunks[0] if len(chunks) == 1 else jnp.concatenate(chunks, axis=0)
    bn_ref[...] = _rms_norm(b_out, gob_ref[...]).astype(BF16)


def _input_projection(x, ln_g, ln_b, mod3, w, cos, sin, tm, with_vrows):
    b, t, d = x.shape
    assert t % tm == 0 and (tm % GMLP_CHUNK == 0 or (tm == t and t < GMLP_CHUNK))
    full = lambda a: pl.BlockSpec(a.shape, lambda bi, i: (0,) * a.ndim)
    tok = lambda width: pl.BlockSpec((None, tm, width), lambda bi, i: (bi, i, 0))
    weights = [w["w_in"], w["g_q"], w["g_kv"], w["w_uq"], w["w_ukv"], w["g_v"], w["b_v"], w["w_s"], w["b_s"],
               w["g_out_b"]]
    out_widths = [(MLA_HEADS * HEAD_PAD, BF16), (MLA_HEADS * HEAD_PAD, BF16), None,
                  (KV_LORA, F32), (QK_ROPE, F32), (GMLP_WIDTH, BF16)]
    if with_vrows:
        out_widths.append((GMLP_WIDTH, F32))
    out_specs = [pl.BlockSpec((None, MLA_WIDTH, tm), lambda bi, i: (bi, 0, i)) if o is None else tok(o[0])
                 for o in out_widths]
    out_shape = [jax.ShapeDtypeStruct((b, MLA_WIDTH, t), BF16) if o is None
                 else jax.ShapeDtypeStruct((b, t, o[0]), o[1]) for o in out_widths]
    return pl.pallas_call(
        functools.partial(_proj_kernel, d_model=d),
        grid=(b, t // tm),
        in_specs=[tok(d), full(ln_g), full(ln_b),
                  pl.BlockSpec((None, 1, mod3.shape[2]), lambda bi, i: (bi, 0, 0))]
                 + [full(a) for a in weights]
                 + [pl.BlockSpec((tm, cos.shape[1]), lambda bi, i: (i, 0))] * 2,
        out_specs=out_specs,
        out_shape=out_shape,
        compiler_params=_cparams(("parallel", "parallel")),
        name="input_projection",
    )(x, ln_g, ln_b, mod3, *weights, cos, sin)


def _cache_kernel(ckv_ref, kr_ref, wukv_ref, dup_ref, k_ref, vt_ref):
    kv = _dot(ckv_ref[...].astype(BF16), wukv_ref[...])
    krd = _dot(kr_ref[...].astype(BF16), dup_ref[...]).astype(BF16)
    nw = MLA_HEADS * QK_NOPE
    for hd in range(MLA_HEADS):
        base = hd * HEAD_PAD
        k_ref[:, base:base + QK_NOPE] = kv[:, hd * QK_NOPE:(hd + 1) * QK_NOPE].astype(BF16)
        k_ref[:, base + QK_NOPE:base + HEAD_PAD] = krd
    vt_ref[...] = _transpose(kv[:, nw:nw + MLA_WIDTH]).astype(BF16)


def _cache_projection(cache_ckv, cache_kr, w_ukv, dup, tm):
    b, p, _ = cache_ckv.shape
    assert p % tm == 0
    tok = lambda width: pl.BlockSpec((None, tm, width), lambda bi, i: (bi, i, 0))
    full = lambda a: pl.BlockSpec(a.shape, lambda bi, i: (0,) * a.ndim)
    return pl.pallas_call(
        _cache_kernel,
        grid=(b, p // tm),
        in_specs=[tok(KV_LORA), tok(QK_ROPE), full(w_ukv), full(dup)],
        out_specs=[tok(MLA_HEADS * HEAD_PAD), pl.BlockSpec((None, MLA_WIDTH, tm), lambda bi, i: (bi, 0, i))],
        out_shape=[jax.ShapeDtypeStruct((b, p, MLA_HEADS * HEAD_PAD), BF16),
                   jax.ShapeDtypeStruct((b, MLA_WIDTH, p), BF16)],
        compiler_params=_cparams(("parallel", "parallel")),
        name="cache_projection",
    )(cache_ckv, cache_kr, w_ukv, dup)


def _attn_kernel(q_ref, *refs, past, segments, tq):
    k_refs = refs[0:2 * len(segments):2]
    vt_refs = refs[1:2 * len(segments):2]
    ga_ref, o_ref = refs[2 * len(segments):]
    i = pl.program_id(1)
    q_first = past + i * tq
    lo = (q_first // CHUNK + 1) * CHUNK
    hi = ((q_first + tq - 1) // CHUNK + 1) * CHUNK
    carry = tuple((jnp.full((1, tq), NEG, F32), jnp.zeros((1, tq), F32), jnp.zeros((V_HEAD, tq), F32))
                  for _ in range(MLA_HEADS))
    for (start, length, tk), k_ref, vt_ref in zip(segments, k_refs, vt_refs):
        n_full = jnp.clip(lo - start, 0, length) // tk
        n_tiles = (jnp.clip(hi - start, 0, length) + tk - 1) // tk
        carry = lax.fori_loop(0, n_full, _attn_step(q_ref, k_ref, vt_ref, q_first, start, tq, tk, False), carry)
        carry = lax.fori_loop(n_full, n_tiles, _attn_step(q_ref, k_ref, vt_ref, q_first, start, tq, tk, True), carry)
    a_t = jnp.concatenate([acc / l for _, l, acc in carry], axis=0)
    o_ref[...] = _rms_norm(_transpose(a_t), ga_ref[...]).astype(BF16)


def _attn_step(q_ref, k_ref, vt_ref, q_first, start, tq, tk, masked):
    def step(j, carry):
        off = 0 if k_ref.shape[0] == tk else pl.multiple_of(j * tk, tk)
        if masked:
            q_chunk = (q_first + lax.broadcasted_iota(jnp.int32, (tk, tq), 1)) // CHUNK
            k_pos = start + off + lax.broadcasted_iota(jnp.int32, (tk, tq), 0)
            visible = k_pos // CHUNK <= q_chunk
        heads = range(MLA_HEADS)
        scores = [_dot_nt(k_ref[pl.ds(off, tk), hd * HEAD_PAD:(hd + 1) * HEAD_PAD],
                          q_ref[:, hd * HEAD_PAD:(hd + 1) * HEAD_PAD]) for hd in heads]
        probs, stats = [], []
        for hd in heads:
            m, l, _ = carry[hd]
            s = scores[hd]
            if masked:
                s = jnp.where(visible, s, NEG)
            m_new = jnp.maximum(m, jnp.max(s, axis=0, keepdims=True))
            alpha = jnp.exp2(m - m_new)
            p = jnp.exp2(s - m_new)
            stats.append((m_new, alpha * l + jnp.sum(p, axis=0, keepdims=True), alpha))
            probs.append(p.astype(BF16))
        new = []
        for hd in heads:
            m_new, l, alpha = stats[hd]
            vth = vt_ref[hd * V_HEAD:(hd + 1) * V_HEAD, pl.ds(off, tk)]
            new.append((m_new, l, alpha * carry[hd][2] + _dot(vth, probs[hd])))
        return tuple(new)
    return step


def _attention(q, kv_segments, g_out_a, past, tq):
    b, t, _ = q.shape
    assert t % tq == 0
    segments, operands, specs = [], [], []
    start = 0
    for k, vt, tk in kv_segments:
        lk = k.shape[1]
        assert lk % tk == 0 and vt.shape[2] == lk and start % CHUNK == 0
        segments.append((start, lk, tk))
        operands += [k, vt]
        specs += [pl.BlockSpec((None, lk, k.shape[2]), lambda bi, i: (bi, 0, 0)),
                  pl.BlockSpec((None, vt.shape[1], lk), lambda bi, i: (bi, 0, 0))]
        start += lk
    assert start == past + t
    return pl.pallas_call(
        functools.partial(_attn_kernel, past=past, segments=tuple(segments), tq=tq),
        grid=(b, t // tq),
        in_specs=[pl.BlockSpec((None, tq, q.shape[2]), lambda bi, i: (bi, i, 0))] + specs
                 + [pl.BlockSpec(g_out_a.shape, lambda bi, i: (0, 0))],
        out_specs=pl.BlockSpec((None, tq, MLA_WIDTH), lambda bi, i: (bi, i, 0)),
        out_shape=jax.ShapeDtypeStruct((b, t, MLA_WIDTH), BF16),
        compiler_params=_cparams(("parallel", "arbitrary")),
        name="attention",
    )(q, *operands, g_out_a)


def _extract_top(s, rounds):
    r_, l_ = s.shape
    ridx = lax.broadcasted_iota(jnp.int32, (r_, l_), 0).astype(F32)
    vidx = lax.broadcasted_iota(jnp.int32, (rounds, l_), 0)
    rank = jnp.full((r_, l_), float(rounds), F32)
    vals = jnp.zeros((rounds, l_), F32)
    for r in range(rounds):
        m = jnp.max(s, axis=0, keepdims=True)
        first = jnp.min(jnp.where(s == m, ridx, float(r_)), axis=0, keepdims=True)
        hit = ridx == first
        rank = jnp.where(hit, float(r), rank)
        s = jnp.where(hit, LOWEST, s)
        vals = jnp.where(vidx == r, m, vals)
    return rank, vals


def _route_kernel(x_ref, an_ref, bn_ref, lng_ref, lnb_ref, mod_ref, wo_ref, l1g_ref, l1b_ref, ws_ref,
                  perm_ref, grp_ref, x1_ref, h2t_ref, rank2_ref, e2_ref, nrow_ref, e1z_ref, sc_ref,
                  cand_ref, *, d_model, alpha, seg):
    d = d_model
    tm = x_ref.shape[0]
    h0 = _layer_norm(x_ref[...], lng_ref[...], lnb_ref[...])
    mix = _dot(an_ref[...], wo_ref[0:MLA_WIDTH, :]) + _dot(bn_ref[...], wo_ref[MLA_WIDTH:, :])
    parts = []
    for sgm in range(tm // seg):
        rows = slice(sgm * seg, (sgm + 1) * seg)
        gt1 = mod_ref[sgm, :, 2 * d:3 * d]
        sh2 = mod_ref[sgm, :, 3 * d:4 * d]
        sc2 = mod_ref[sgm, :, 4 * d:5 * d]
        x1 = _layer_norm(alpha * h0[rows] + (1.0 + gt1) * mix[rows], l1g_ref[...], l1b_ref[...])
        x1_ref[rows, :] = x1
        parts.append(x1 * (1.0 + sc2) + sh2)
    h2 = parts[0] if len(parts) == 1 else jnp.concatenate(parts, axis=0)
    h2t = h2.T.astype(BF16)
    h2t_ref[...] = h2t
    rows = PEER_HEADS * N_KEYS
    x1a = _dot(ws_ref[0], h2t)
    x2a = _dot(ws_ref[1], h2t)
    for g in range(tm // LANES):
        sc_ref[0, g] = x1a[:, g * LANES:(g + 1) * LANES]
        sc_ref[1, g] = x2a[:, g * LANES:(g + 1) * LANES]
    x1 = [x1a[k * PEER_HEADS:(k + 1) * PEER_HEADS, :] for k in range(N_KEYS)]
    x2 = [x2a[k * PEER_HEADS:(k + 1) * PEER_HEADS, :] for k in range(N_KEYS)]
    a, gap1 = _top_sorted(x1)
    b, gap2 = _top_sorted(x2)
    cand = [a[k1] + b[k2] for k1, cnt in enumerate(CAND_COUNTS) for k2 in range(cnt)]
    lowest = jnp.full((PEER_HEADS, tm), LOWEST, F32)
    top, gap3 = _top_sorted(cand + [lowest] * (CAND_ROWS - len(cand)))
    gap = jnp.minimum(jnp.minimum(gap1, gap2), gap3)
    thr = top[PEER_TOPK - 1]
    z = None
    n = []
    off = 0
    for k1, cnt in enumerate(CAND_COUNTS):
        nk = None
        for c in cand[off:off + cnt]:
            sel = c >= thr
            one = jnp.where(sel, 1.0, 0.0)
            nk = one if nk is None else nk + one
            e = jnp.where(sel, jnp.exp(c - cand[0]), 0.0)
            z = e if z is None else z + e
        n.append(nk)
        off += cnt
    zinv = 1.0 / z
    x1k = x1a.reshape(N_KEYS, PEER_HEADS, tm)
    x2k = x2a.reshape(N_KEYS, PEER_HEADS, tm)
    nrow = jnp.zeros_like(x1k)
    rank2 = jnp.full(x2k.shape, float(PEER_TOPK), F32)
    for r in range(PEER_TOPK):
        nrow = jnp.where(x1k == a[r][None], n[r][None], nrow)
        rank2 = jnp.where(x2k == b[r][None], float(r), rank2)

    blk = BF16_ROWS * PEER_HEADS
    for out_ref, v in ((rank2_ref, rank2), (e2_ref, jnp.exp(x2k - b[0][None])), (nrow_ref, nrow),
                       (e1z_ref, jnp.exp(x1k - a[0][None]) * zinv[None])):
        tab = v.reshape(rows, tm).astype(BF16)
        for kb in range(rows // blk):
            out_ref[kb * blk:(kb + 1) * blk, :] = _dot(perm_ref[...], tab[kb * blk:(kb + 1) * blk, :]).astype(BF16)

    pad0 = sum(CAND_COUNTS) // SUBLANES * SUBLANES
    cand_ref[pad0:CAND_ROWS, :] = jnp.full((CAND_ROWS - pad0, tm), LOWEST, F32)
    for hd in range(PEER_HEADS):
        @pl.when(jnp.min(gap[hd:hd + 1, :]) <= 0.0)
        def _(hd=hd):
            _route_head_by_extraction(hd, sc_ref, grp_ref, rank2_ref, e2_ref, nrow_ref, e1z_ref, cand_ref)


def _top_sorted(vals):
    x = list(vals)
    n = len(x)
    k = PEER_TOPK

    def order(i, j):
        hi = jnp.maximum(x[i], x[j])
        x[j] = jnp.minimum(x[i], x[j])
        x[i] = hi

    for base in range(0, n, k):
        size = 2
        while size <= k:
            j = size // 2
            while j >= 1:
                for i in range(k):
                    l = i ^ j
                    if l > i:
                        if i & size == 0:
                            order(base + i, base + l)
                        else:
                            order(base + l, base + i)
                j //= 2
            size *= 2
    dropped = None
    step = k
    while step < n:
        for base in range(0, n, 2 * step):
            for i in range(k):
                lo = jnp.minimum(x[base + i], x[base + step + k - 1 - i])
                x[base + i] = jnp.maximum(x[base + i], x[base + step + k - 1 - i])
                dropped = lo if dropped is None else jnp.maximum(dropped, lo)
            j = k // 2
            while j >= 1:
                for i in range(k):
                    l = i ^ j
                    if l > i:
                        order(base + i, base + l)
                j //= 2
        step *= 2
    top = x[:k]
    gap = top[k - 1] - dropped
    for r in range(k - 1):
        gap = jnp.minimum(gap, top[r] - top[r + 1])
    return top, gap


def _route_head_by_extraction(hd, sc_ref, grp_ref, rank2_ref, e2_ref, nrow_ref, e1z_ref, cand_ref):
    lane_tiles = sc_ref.shape[1]
    tm = lane_tiles * LANES
    head_rows = pl.ds(hd, N_KEYS, stride=PEER_HEADS)
    s1 = jnp.concatenate([sc_ref[0, g, head_rows, :] for g in range(lane_tiles)], axis=1)
    s2 = jnp.concatenate([sc_ref[1, g, head_rows, :] for g in range(lane_tiles)], axis=1)
    rank1, a = _extract_top(s1, PEER_TOPK)
    rank2, bvals = _extract_top(s2, PEER_TOPK)
    off = 0
    for k1, cnt in enumerate(CAND_COUNTS):
        cand_ref[off:off + cnt, :] = a[k1:k1 + 1, :] + bvals[0:cnt, :]
        off += cnt
    cand = cand_ref[...]
    crank, _ = _extract_top(cand, PEER_TOPK)
    sel = crank < float(PEER_TOPK)
    z = jnp.sum(jnp.where(sel, jnp.exp(cand - cand[0:1, :]), 0.0), axis=0, keepdims=True)
    n = _dot(grp_ref[...], jnp.where(sel, 1.0, 0.0).astype(BF16))
    nrow = jnp.zeros((N_KEYS, tm), F32)
    for k1 in range(PEER_TOPK):
        nrow = jnp.where(rank1 == float(k1), n[k1:k1 + 1, :], nrow)
    blk = BF16_ROWS * PEER_HEADS
    for out_ref, v in ((rank2_ref, rank2), (e2_ref, jnp.exp(s2 - bvals[0:1, :])), (nrow_ref, nrow),
                       (e1z_ref, jnp.exp(s1 - a[0:1, :]) / z)):
        vb = v.astype(BF16)
        for kb in range(N_KEYS // BF16_ROWS):
            out_ref[kb * blk + hd * BF16_ROWS:kb * blk + (hd + 1) * BF16_ROWS, :] = (
                vb[kb * BF16_ROWS:(kb + 1) * BF16_ROWS, :])


def _route(x2d, an2d, bn2d, ln_g, ln_b, mod3, w, tm, seg, t_batch, alpha):
    n, d = x2d.shape
    assert n % tm == 0 and tm % seg == 0 and tm % LANES == 0 and PEER_HEADS == SUBLANES
    nseg = tm // seg
    rows = PEER_HEADS * N_KEYS
    tok = lambda width: pl.BlockSpec((tm, width), lambda i: (i, 0))
    full = lambda a: pl.BlockSpec(a.shape, lambda i: (0,) * a.ndim)
    tbl = pl.BlockSpec((rows, tm), lambda i: (0, i))
    weights = [w["w_o"], w["ln1_g"], w["ln1_b"], w["w_score"], w["key_perm"], w["cand_groups"]]
    return pl.pallas_call(
        functools.partial(_route_kernel, d_model=d, alpha=alpha, seg=seg),
        grid=(n // tm,),
        in_specs=[tok(d), tok(MLA_WIDTH), tok(GMLP_WIDTH), full(ln_g), full(ln_b),
                  pl.BlockSpec((nseg, 1, mod3.shape[2]), lambda i: ((i * tm // t_batch) // nseg, 0, 0))]
                 + [full(a) for a in weights],
        out_specs=[tok(d), pl.BlockSpec((d, tm), lambda i: (0, i)), tbl, tbl, tbl, tbl],
        out_shape=[jax.ShapeDtypeStruct((n, d), F32), jax.ShapeDtypeStruct((d, n), BF16)]
                  + [jax.ShapeDtypeStruct((rows, n), BF16)] * 4,
        scratch_shapes=[pltpu.VMEM((2, tm // LANES, rows, LANES), F32), pltpu.VMEM((CAND_ROWS, tm), F32)],
        compiler_params=_cparams(("parallel",)),
        name="route",
    )(x2d, an2d, bn2d, ln_g, ln_b, mod3, *weights)


def _peer_kernel(h2t_ref, rank2_ref, e2_ref, nrow_ref, e1z_ref, u_ref, vt_ref, x1_ref, mod_ref, g_ref, b_ref,
                 y_ref, a_ref, o_ref, *, tw, n_chunks, alpha, seg):
    s = pl.program_id(0)
    tb = h2t_ref.shape[1]
    rows_per_chunk = u_ref.shape[0] // N_KEYS
    cur = s % 2
    prev = 1 - cur

    @pl.when(s == 0)
    def _():
        a_ref[1] = jnp.zeros(a_ref.shape[1:], BF16)

    @pl.when(jnp.maximum(s - 1, 0) % n_chunks == 0)
    def _():
        o_ref[...] = jnp.zeros_like(o_ref)

    def subtile(ts, carry):
        off = pl.multiple_of(ts * tw, tw)
        o_ref[:, pl.ds(off, tw)] += _dot(vt_ref[...], a_ref[prev, :, pl.ds(off, tw)])
        ht = h2t_ref[:, pl.ds(off, tw)]
        nr_rows = [nrow_ref[0, hd, :, pl.ds(off, tw)].astype(F32) for hd in range(PEER_HEADS)]
        e1_rows = [e1z_ref[0, hd, :, pl.ds(off, tw)].astype(F32) for hd in range(PEER_HEADS)]
        for p in range(rows_per_chunk):
            s = _dot(u_ref[p * N_KEYS:(p + 1) * N_KEYS, :], ht)
            wgt = jnp.zeros((N_KEYS // BF16_ROWS, BF16_ROWS, tw), BF16)
            for hd in range(PEER_HEADS):
                nr = jnp.broadcast_to(nr_rows[hd][p:p + 1, :], (BF16_ROWS, tw)).astype(BF16)
                e1 = jnp.broadcast_to(e1_rows[hd][p:p + 1, :], (BF16_ROWS, tw)).astype(BF16)
                r2 = rank2_ref[:, hd, :, pl.ds(off, tw)]
                e2 = e2_ref[:, hd, :, pl.ds(off, tw)]
                wgt = wgt + jnp.where(r2 < nr[None], e2, jnp.zeros_like(e2)) * e1[None]
            a_ref[cur, p * N_KEYS:(p + 1) * N_KEYS, pl.ds(off, tw)] = (
                _gelu(s).astype(BF16) * wgt.reshape(N_KEYS, tw))
        return carry

    lax.fori_loop(0, tb // tw, subtile, 0)

    @pl.when((s > 0) & (s % n_chunks == 0))
    def _():
        d = o_ref.shape[0]
        f = _transpose(o_ref[...])
        for sgm in range(tb // seg):
            rows = slice(sgm * seg, (sgm + 1) * seg)
            gt2 = mod_ref[sgm, :, 5 * d:6 * d]
            y_ref[rows, :] = _layer_norm(alpha * x1_ref[rows, :] + (1.0 + gt2) * f[rows], g_ref[...], b_ref[...])


def _peer_mix(h2t, rank2, e2, nrow, e1z, u_b, vt_b, x1, mod3, ln_g, ln_b, tb, tw, ec, seg, t_batch, alpha):
    d, n = h2t.shape
    ne = u_b.shape[0]
    assert n % tb == 0 and tb % tw == 0 and ne % ec == 0 and ec == BF16_ROWS * N_KEYS
    groups = N_KEYS // BF16_ROWS
    shape4 = (groups, PEER_HEADS, BF16_ROWS, n)
    rank2, e2, nrow, e1z = (a.reshape(shape4) for a in (rank2, e2, nrow, e1z))
    nc = ne // ec
    nb = n // tb
    blk_in = lambda s: jnp.minimum(s // nc, nb - 1)
    blk_out = lambda s: jnp.maximum(s - 1, 0) // nc
    tbl = pl.BlockSpec((groups, PEER_HEADS, BF16_ROWS, tb), lambda s: (0, 0, 0, blk_in(s)))
    row = pl.BlockSpec((1, PEER_HEADS, BF16_ROWS, tb), lambda s: (s % nc, 0, 0, blk_in(s)))
    assert tb % seg == 0 and t_batch % seg == 0
    nseg = tb // seg
    return pl.pallas_call(
        functools.partial(_peer_kernel, tw=tw, n_chunks=nc, alpha=alpha, seg=seg),
        grid=(nb * nc + 1,),
        in_specs=[pl.BlockSpec((d, tb), lambda s: (0, blk_in(s))), tbl, tbl, row, row,
                  pl.BlockSpec((ec, d), lambda s: (s % nc, 0)),
                  pl.BlockSpec((d, ec), lambda s: (0, jnp.maximum(s - 1, 0) % nc)),
                  pl.BlockSpec((tb, d), lambda s: (blk_out(s), 0)),
                  pl.BlockSpec((nseg, 1, mod3.shape[2]), lambda s: ((blk_out(s) * tb // t_batch) // nseg, 0, 0)),
                  pl.BlockSpec(ln_g.shape, lambda s: (0, 0)), pl.BlockSpec(ln_b.shape, lambda s: (0, 0))],
        out_specs=pl.BlockSpec((tb, d), lambda s: (blk_out(s), 0)),
        out_shape=jax.ShapeDtypeStruct((n, d), F32),
        scratch_shapes=[pltpu.VMEM((2, ec, tb), BF16), pltpu.VMEM((d, tb), F32)],
        compiler_params=_cparams(("arbitrary",)),
        name="peer_mix",
    )(h2t, rank2, e2, nrow, e1z, u_b, vt_b, x1, mod3, ln_g, ln_b)


def _combine_kernel(sk_ref, wq_ref, o_ref):
    o_ref[...] = _dot(sk_ref[...], wq_ref[...]).astype(BF16)


def _combine_score_weights(sub_keys_bd, w_pq_t):
    sides, rows, inner = sub_keys_bd.shape
    d = w_pq_t.shape[1]
    return pl.pallas_call(
        _combine_kernel,
        grid=(sides,),
        in_specs=[pl.BlockSpec((None, rows, inner), lambda s: (s, 0, 0)),
                  pl.BlockSpec((inner, d), lambda s: (s, 0))],
        out_specs=pl.BlockSpec((None, rows, d), lambda s: (s, 0, 0)),
        out_shape=jax.ShapeDtypeStruct((sides, rows, d), BF16),
        compiler_params=_cparams(("parallel",)),
        name="score_weights",
    )(sub_keys_bd, w_pq_t)


def _rope_tables(past, t):
    half = QK_ROPE // 2
    inv_freq = jnp.power(jnp.float32(ROPE_BASE), -jnp.arange(half, dtype=F32) * (2.0 / QK_ROPE))
    pos = past + jnp.arange(t, dtype=jnp.int32)
    ang = pos.astype(F32)[:, None] * inv_freq[None, :]
    cos = jnp.cos(ang)
    sin = jnp.sin(ang)
    cos_t = jnp.tile(jnp.concatenate([cos, cos], axis=1), (1, MLA_HEADS))
    sin_t = jnp.tile(jnp.concatenate([-sin, sin], axis=1), (1, MLA_HEADS))
    return cos_t, sin_t


def _swap_halves(w):
    half = QK_ROPE // 2
    return jnp.concatenate([w[..., half:], w[..., :half]], axis=-1)


def _prepare_layer(w_in, g_q, g_kv, w_uq, w_uk, w_uv, g_v, b_v, w_s, b_s, g_out_a, g_out_b, w_o,
                   ln1_g, ln1_b, w_pq, sub_keys, peer_u, peer_v, ln2_g, ln2_b):
    o1 = Q_LORA
    o2 = o1 + KV_LORA
    o3 = o2 + QK_ROPE
    o4 = o3 + GMLP_WIDTH
    kr_w = w_in[:, o2:o3]
    kr_sw = _swap_halves(kr_w)
    w_in_x = jnp.concatenate([w_in[:, :o2], w_in[:, o3:o4], w_in[:, o4:], kr_w, kr_w, kr_sw, kr_sw], axis=1)
    uq = w_uq.reshape(Q_LORA, MLA_HEADS, QK_NOPE + QK_ROPE)
    uq_rope = uq[:, :, QK_NOPE:]
    w_uq_x = jnp.concatenate([uq[:, :, :QK_NOPE].reshape(Q_LORA, -1), uq_rope.reshape(Q_LORA, -1),
                              _swap_halves(uq_rope).reshape(Q_LORA, -1)], axis=1)
    row = lambda a: a.reshape(1, -1)
    groups = []
    for k1, cnt in enumerate(CAND_COUNTS):
        groups += [k1] * cnt
    groups += [-1] * (CAND_ROWS - len(groups))
    cand_groups = (jnp.arange(PEER_TOPK)[:, None] == jnp.asarray(groups)[None, :]).astype(BF16)
    half = D_KEY // 2
    w_pq_t = w_pq.reshape(-1, PEER_HEADS, 2, half).transpose(2, 1, 3, 0).reshape(2 * PEER_HEADS * half, -1)
    sub_keys_bd = jnp.einsum("hpkd,hg->pkhgd", sub_keys, jnp.eye(PEER_HEADS, dtype=sub_keys.dtype))
    sub_keys_bd = sub_keys_bd.reshape(2, N_KEYS * PEER_HEADS, PEER_HEADS * half)
    src = jnp.arange(BF16_ROWS * PEER_HEADS)
    key_perm = ((src % BF16_ROWS) * PEER_HEADS + src // BF16_ROWS)[:, None] == src[None, :]
    return dict(
        w_score=_combine_score_weights(sub_keys_bd.astype(BF16), w_pq_t.astype(BF16)), key_perm=key_perm.astype(BF16),
        w_in=w_in_x.astype(BF16), g_q=row(g_q), g_kv=row(g_kv), w_uq=w_uq_x.astype(BF16),
        w_ukv=jnp.concatenate([w_uk, w_uv], axis=1).astype(BF16), g_v=row(g_v), b_v=row(b_v), w_s=w_s,
        b_s=jnp.broadcast_to(b_s[:, :, None], b_s.shape + (GMLP_HEAD_DIM,)), g_out_a=row(g_out_a),
        g_out_b=row(g_out_b), w_o=w_o.astype(BF16), ln1_g=row(ln1_g), ln1_b=row(ln1_b),
        cand_groups=cand_groups, peer_u=peer_u.astype(BF16), peer_vt=peer_v.T.astype(BF16),
        ln2_g=row(ln2_g), ln2_b=row(ln2_b),
        kr_dup=jnp.concatenate([jnp.eye(QK_ROPE, dtype=BF16)] * 2, axis=1),
    )


def _pick_tile(n, prefs):
    for p in prefs:
        if n % p == 0:
            return p
    raise ValueError(f"no tile for {n}")


def _layer_group(x, mod, cache_ckv, cache_kr, w, ln_g, ln_b, alpha):
    b, t, d = x.shape
    n = b * t
    past = 0 if cache_ckv is None else cache_ckv.shape[1]
    mod3 = mod.reshape(b, 1, mod.shape[1])
    cos, sin = _rope_tables(past, t)

    tm = _pick_tile(t, (256, 128)) if t >= GMLP_CHUNK else t
    outs = _input_projection(x, ln_g, ln_b, mod3, w, cos, sin, tm, cache_ckv is not None)
    q, k_new, v_new, ckv, kr, bn = outs[:6]
    v_rows = outs[6] if cache_ckv is not None else None

    tq = _pick_tile(t, (512, 256, 128, 64))
    kv_segments = [(k_new, v_new, _pick_tile(t, (512, 256, 128, 64)))]
    if cache_ckv is not None:
        k_c, v_c = _cache_projection(cache_ckv, cache_kr, w["w_ukv"], w["kr_dup"], _pick_tile(past, (512, 256, 128)))
        kv_segments.insert(0, (k_c, v_c, _pick_tile(past, (512, 256, 128))))
    an = _attention(q, kv_segments, w["g_out_a"], past, tq)

    tr = _pick_tile(n, (256, 128))
    seg = min(t, tr)
    assert tr % seg == 0 and t % seg == 0
    x1, h2t, rank2, e2, nrow, e1z = _route(x.reshape(n, d), an.reshape(n, -1), bn.reshape(n, -1), ln_g, ln_b,
                                          mod3, w, tr, seg, t, alpha)
    tb = _pick_tile(n, (512, 256))
    y = _peer_mix(h2t, rank2, e2, nrow, e1z, w["peer_u"], w["peer_vt"], x1, mod3, w["ln2_g"], w["ln2_b"],
                  tb, 256, BF16_ROWS * N_KEYS, min(t, tb), t, alpha)
    return y.reshape(b, t, d), ckv, kr, v_rows


def kernel(x_prompt, x_sample, cache_ckv, cache_krope, c_prompt, c_sample, ln_in_g, ln_in_b, w_ada, b_ada, w_in,
           g_q, g_kv, w_uq, w_uk, w_uv, g_v, b_v, w_s, b_s, g_out_a, g_out_b, w_o, ln1_g, ln1_b, w_pq, sub_keys,
           peer_u, peer_v, ln2_g, ln2_b):
    depth = w_ada.shape[0]
    assert depth == 1, "the entry LayerNorm is fused into the first layer's kernels"
    alpha = (2 * depth) ** 0.25
    bp = c_prompt.shape[0]
    bs = c_sample.shape[0]
    c_all = jnp.concatenate([c_prompt, c_sample], axis=0)
    c_all = jnp.pad(c_all, ((0, (-c_all.shape[0]) % 8), (0, 0)))
    ln_g = ln_in_g.reshape(1, -1)
    ln_b = ln_in_b.reshape(1, -1)
    hp, hs = x_prompt, x_sample
    ckv_p, kr_p, ckv_s, kr_s, gv_s = [], [], [], [], []
    for l in range(depth):
        w = _prepare_layer(w_in[l], g_q[l], g_kv[l], w_uq[l], w_uk[l], w_uv[l], g_v[l], b_v[l], w_s[l], b_s[l],
                           g_out_a[l], g_out_b[l], w_o[l], ln1_g[l], ln1_b[l], w_pq[l], sub_keys[l], peer_u[l],
                           peer_v[l], ln2_g[l], ln2_b[l])
        mod = _ada_mod(c_all, w_ada[l], b_ada[l])
        hp, ckv_new_p, kr_new_p, _ = _layer_group(hp, mod[:bp], None, None, w, ln_g, ln_b, alpha)
        hs, ckv_new_s, kr_new_s, v_new_s = _layer_group(hs, mod[bp:bp + bs], cache_ckv[l], cache_krope[l], w,
                                                       ln_g, ln_b, alpha)
        ckv_p.append(ckv_new_p)
        kr_p.append(kr_new_p)
        ckv_s.append(ckv_new_s)
        kr_s.append(kr_new_s)
        gv_s.append(v_new_s)
    return (hp, hs, jnp.stack(ckv_p), jnp.stack(kr_p), jnp.stack(ckv_s), jnp.stack(kr_s), jnp.stack(gv_s))
```
